```python
import math
import jax, jax.numpy as jnp
from jax import lax
import numpy as np

D_MODEL = 1024
BATCH = 8
SEQ = 4096
DEPTH = 2

GRID_W = 64
CTX_LEN = 256

D_BRANCH = D_MODEL // 2
N_BRANCH = 3
D_LRU = D_BRANCH
N_LRU_BLOCKS = 8
LRU_BLOCK = D_LRU // N_LRU_BLOCKS
LRU_CONV = 4
LRU_C = 8.0
D_NA = D_BRANCH
NA_HEAD_DIM = 64
NA_HEADS = D_NA // NA_HEAD_DIM
NA_KH = 8
NA_KW = 16
D_HY = D_BRANCH
HY_CONV = 3
HY_POS_BANDS = 16
HY_EMB = 1 + 2 * HY_POS_BANDS
HY_HID = 64
HY_DECAY_TARGET = 1e-2
HY_FAST_DECAY = 0.3
HY_SLOW_DECAY = 1.5

ALPHA = (2 * DEPTH) ** 0.25
BETA = (8 * DEPTH) ** -0.25
LN_EPS = 1e-5

IN_WIDTHS = (D_LRU, D_LRU, D_NA, D_NA, D_NA, D_NA, 3 * D_HY, D_HY, D_MODEL, D_MODEL, D_MODEL)
D_IN = sum(IN_WIDTHS)
IN_SPLITS = tuple(int(s) for s in np.cumsum(IN_WIDTHS)[:-1])

kernel_name = "hybrid_rglru_natten_hyena_diffusion_trunk"


def layer_norm(x, g=None, b=None):
    xf = x.astype(jnp.float32)
    mu = jnp.mean(xf, -1, keepdims=True)
    var = jnp.mean(jnp.square(xf - mu), -1, keepdims=True)
    y = (xf - mu) * lax.rsqrt(var + LN_EPS)
    if g is not None:
        y = y * g.astype(jnp.float32) + b.astype(jnp.float32)
    return y.astype(x.dtype)


def dwconv_centred(x, w, b):
    k_w, ch = w.shape
    left = k_w // 2
    y = lax.conv_general_dilated(
        x, w[:, None, :].astype(x.dtype), window_strides=(1,), padding=[(left, k_w - 1 - left)],
        dimension_numbers=('NWC', 'WIO', 'NWC'), feature_group_count=ch)
    return y + b


def rglru_coeffs(x, w_a, b_a, w_x, b_x, lam):
    bsz, length, _ = x.shape
    xh = x.reshape(bsz, length, N_LRU_BLOCKS, LRU_BLOCK)
    r = jax.nn.sigmoid(jnp.einsum('blgi,gij->blgj', xh, w_a).reshape(bsz, length, D_LRU) + b_a)
    i = jax.nn.sigmoid(jnp.einsum('blgi,gij->blgj', xh, w_x).reshape(bsz, length, D_LRU) + b_x)
    log_a = -LRU_C * r.astype(jnp.float32) * jax.nn.softplus(-lam.astype(jnp.float32))
    a = jnp.exp(log_a)
    b = jnp.sqrt(-jnp.expm1(2.0 * log_a)) * (i * x).astype(jnp.float32)
    return a, b


def linear_scan(a, b, h0):
    b = b.at[:, 0].add(a[:, 0] * h0)

    def combine(p, q):
        a1, b1 = p
        a2, b2 = q
        return a1 * a2, a2 * b1 + b2

    _, h = lax.associative_scan(combine, (a, b), axis=1)
    return h


def rglru_bidir(x_ctx, x_lat, w_a, b_a, w_x, b_x, lam):
    y_ctx = jnp.zeros(x_ctx.shape, jnp.float32)
    y_lat = jnp.zeros(x_lat.shape, jnp.float32)
    for d in range(2):
        xc = x_ctx if d == 0 else jnp.flip(x_ctx, 1)
        xl = x_lat if d == 0 else jnp.flip(x_lat, 1)
        ac, bc = rglru_coeffs(xc, w_a[d], b_a[d], w_x[d], b_x[d], lam[d])
        hc = linear_scan(ac, bc, jnp.zeros_like(bc[:, 0]))
        al, bl = rglru_coeffs(xl, w_a[d], b_a[d], w_x[d], b_x[d], lam[d])
        hl = linear_scan(al, bl, hc[:, -1])
        if d == 1:
            hc = jnp.flip(hc, 1)
            hl = jnp.flip(hl, 1)
        y_ctx = y_ctx + hc
        y_lat = y_lat + hl
    return y_ctx.astype(x_ctx.dtype), y_lat.astype(x_lat.dtype)


def neighbourhood_attention(q, k, v, k_ctx, v_ctx, rpb):
    bsz, length, heads, dh = q.shape
    rows = length // GRID_W
    kh = min(NA_KH, rows)
    kw = NA_KW
    kg = k.reshape(bsz, rows, GRID_W, heads, dh)
    vg = v.reshape(bsz, rows, GRID_W, heads, dh)
    qg = (q * dh ** -0.5).reshape(bsz, rows, GRID_W, heads, dh).transpose(1, 0, 2, 3, 4)
    cols = jnp.arange(GRID_W)
    col_start = jnp.clip(cols - kw // 2, 0, GRID_W - kw)
    col_ok = (cols[None, :] >= col_start[:, None]) & (cols[None, :] < col_start[:, None] + kw)
    dc_idx = jnp.clip(cols[None, :] - cols[:, None], -(kw - 1), kw - 1) + (NA_KW - 1)
    n_loc = kh * GRID_W

    def one_row(args):
        r, q_row = args
        r0 = jnp.clip(r - kh // 2, 0, rows - kh)
        k_band = lax.dynamic_slice_in_dim(kg, r0, kh, axis=1)
        v_band = lax.dynamic_slice_in_dim(vg, r0, kh, axis=1)
        dr_idx = r0 + jnp.arange(kh) - r + (NA_KH - 1)
        bias = rpb[:, dr_idx[None, :, None], dc_idx[:, None, :]].astype(jnp.float32)
        bias = jnp.where(col_ok[None, :, None, :], bias, -jnp.inf)
        s_loc = jnp.einsum('bqhd,brkhd->bhqrk', q_row, k_band).astype(jnp.float32) + bias
        s_ctx = jnp.einsum('bqhd,bchd->bhqc', q_row, k_ctx).astype(jnp.float32)
        s = jnp.concatenate([s_loc.reshape(bsz, heads, GRID_W, n_loc), s_ctx], -1)
        p = jax.nn.softmax(s, axis=-1).astype(v.dtype)
        p_loc = p[..., :n_loc].reshape(bsz, heads, GRID_W, kh, GRID_W)
        p_ctx = p[..., n_loc:]
        return (jnp.einsum('bhqrk,brkhd->bqhd', p_loc, v_band)
                + jnp.einsum('bhqc,bchd->bqhd', p_ctx, v_ctx))

    out = lax.map(one_row, (jnp.arange(rows), qg))
    return out.transpose(1, 0, 2, 3, 4).reshape(bsz, length, heads * dh)


def context_attention(q, k, v):
    s = jnp.einsum('bqhd,bkhd->bhqk', q * NA_HEAD_DIM ** -0.5, k).astype(jnp.float32)
    p = jax.nn.softmax(s, axis=-1).astype(v.dtype)
    o = jnp.einsum('bhqk,bkhd->bqhd', p, v)
    return o.reshape(o.shape[0], o.shape[1], -1)


def hyena_filters(length, w1, b1, freq1, w2, b2, freq2, w3):
    f32 = jnp.float32
    t = jnp.arange(length, dtype=f32) / length
    ang = 2.0 * math.pi * t[:, None] * jnp.arange(1, HY_POS_BANDS + 1, dtype=f32)
    z = jnp.concatenate([t[:, None], jnp.cos(ang), jnp.sin(ang)], -1)
    hid = jnp.sin(freq1.astype(f32) * (z @ w1.astype(f32) + b1.astype(f32)))
    hid = jnp.sin(freq2.astype(f32) * (hid @ w2.astype(f32) + b2.astype(f32)))
    filt = (hid @ w3.astype(f32)).reshape(length, 2, D_HY)
    deltas = jnp.abs(jnp.linspace(math.log(HY_DECAY_TARGET) / HY_FAST_DECAY,
                                  math.log(HY_DECAY_TARGET) / HY_SLOW_DECAY, D_HY, dtype=f32))
    filt = filt * jnp.exp(-t[:, None] * deltas)[:, None, :]
    return filt[:, 0], filt[:, 1]


def long_conv_bidir(u, h_fwd, h_bwd):
    length = u.shape[1]
    kern = jnp.concatenate([h_fwd, jnp.zeros_like(h_fwd[:1]), jnp.flip(h_bwd[1:], 0)], 0)
    u_f = jnp.fft.rfft(u.astype(jnp.float32), n=2 * length, axis=1)
    k_f = jnp.fft.rfft(kern, n=2 * length, axis=0)
    return jnp.fft.irfft(u_f * k_f[None], n=2 * length, axis=1)[:, :length]


def hyena_mix(z, conv_w, conv_b, h_fwd, h_bwd, d_skip):
    z = dwconv_centred(z, conv_w, conv_b)
    x0, x1, v = jnp.split(z, 3, axis=-1)
    w = x1 * v
    y = long_conv_bidir(w, h_fwd, h_bwd).astype(w.dtype) + d_skip * w
    return x0 * y


def merge_branches(ya, yb, yc, a_gate, b_gate, c_gate, g_a, g_b, g_c, w_branch_out, w_out):
    m = (jax.nn.sigmoid(g_a) * ((ya * jax.nn.silu(a_gate)) @ w_branch_out[0])
         + jax.nn.sigmoid(g_b) * ((yb * jax.nn.silu(b_gate)) @ w_branch_out[1])
         + jax.nn.sigmoid(g_c) * ((yc * jax.nn.silu(c_gate)) @ w_branch_out[2]))
    return m @ w_out


def _heads(t):
    return t.reshape(t.shape[0], t.shape[1], NA_HEADS, NA_HEAD_DIM)


def setup_inputs(seed: int = 0) -> dict:
    key = jax.random.key(seed)
    ks = iter(jax.random.split(key, 40))
    f32 = jnp.float32

    def nrm(shape, s):
        return jax.random.normal(next(ks), shape, f32) * s

    a_c = jax.random.uniform(next(ks), (DEPTH, 2, D_LRU), f32, 0.9, 0.999)
    a0 = a_c ** (1.0 / LRU_C)
    lru_lambda = jnp.log(a0) - jnp.log1p(-a0)
    return {
        "x": nrm((BATCH, SEQ, D_MODEL), 1.0),
        "c": nrm((BATCH, D_MODEL), 1.0),
        "ctx": nrm((BATCH, CTX_LEN, D_MODEL), 1.0),
        "c_ctx": nrm((D_MODEL,), 1.0),
        "w_mod": nrm((DEPTH, D_MODEL, 3 * D_MODEL), 0.5 * D_MODEL ** -0.5),
        "b_mod": nrm((DEPTH, 3 * D_MODEL), 0.02),
        "w_in": nrm((DEPTH, D_MODEL, D_IN), D_MODEL ** -0.5),
        "lru_conv_w": nrm((DEPTH, LRU_CONV, D_LRU), LRU_CONV ** -0.5),
        "lru_conv_b": nrm((DEPTH, D_LRU), 0.02),
        "lru_w_a": nrm((DEPTH, 2, N_LRU_BLOCKS, LRU_BLOCK, LRU_BLOCK), LRU_BLOCK ** -0.5),
        "lru_b_a": nrm((DEPTH, 2, D_LRU), 0.02),
        "lru_w_x": nrm((DEPTH, 2, N_LRU_BLOCKS, LRU_BLOCK, LRU_BLOCK), LRU_BLOCK ** -0.5),
        "lru_b_x": nrm((DEPTH, 2, D_LRU), 0.02),
        "lru_lambda": lru_lambda,
        "na_rpb": nrm((DEPTH, NA_HEADS, 2 * NA_KH - 1, 2 * NA_KW - 1), 0.1),
        "hy_conv_w": nrm((DEPTH, HY_CONV, 3 * D_HY), HY_CONV ** -0.5),
        "hy_conv_b": nrm((DEPTH, 3 * D_HY), 0.02),
        "hy_w1": nrm((DEPTH, HY_EMB, HY_HID), HY_EMB ** -0.5),
        "hy_b1": nrm((DEPTH, HY_HID), 0.02),
        "hy_freq1": 1.0 + nrm((DEPTH, HY_HID), 0.02),
        "hy_w2": nrm((DEPTH, HY_HID, HY_HID), HY_HID ** -0.5),
        "hy_b2": nrm((DEPTH, HY_HID), 0.02),
        "hy_freq2": 1.0 + nrm((DEPTH, HY_HID), 0.02),
        "hy_w3": nrm((DEPTH, HY_HID, 2 * D_HY), 0.05 * HY_HID ** -0.5),
        "hy_d_skip": nrm((DEPTH, D_HY), 1.0),
        "w_branch_out": nrm((DEPTH, N_BRANCH, D_BRANCH, D_MODEL), BETA * D_BRANCH ** -0.5),
        "w_out": nrm((DEPTH, D_MODEL, D_MODEL), BETA * D_MODEL ** -0.5),
        "ln_g": 1.0 + nrm((DEPTH, D_MODEL), 0.02),
        "ln_b": nrm((DEPTH, D_MODEL), 0.02),
    }


def reference(x, c, ctx, c_ctx, w_mod, b_mod, w_in, lru_conv_w, lru_conv_b, lru_w_a, lru_b_a,
              lru_w_x, lru_b_x, lru_lambda, na_rpb, hy_conv_w, hy_conv_b, hy_w1, hy_b1, hy_freq1,
              hy_w2, hy_b2, hy_freq2, hy_w3, hy_d_skip, w_branch_out, w_out, ln_g, ln_b):
    h, hc = x, ctx
    seq_len = x.shape[1]
    ctx_len = ctx.shape[1]
    for l in range(DEPTH):
        last = l == DEPTH - 1
        mod = jax.nn.silu(c) @ w_mod[l] + b_mod[l]
        mod_x = jax.nn.silu(c_ctx) @ w_mod[l] + b_mod[l]
        shift, scale, gate = jnp.split(mod[:, None, :], 3, axis=-1)
        shift_x, scale_x, gate_x = jnp.split(mod_x, 3)
        u = layer_norm(h) * (1.0 + scale) + shift
        ux = layer_norm(hc) * (1.0 + scale_x) + shift_x

        (a_in, a_gate, q, k, v, b_gate, c_in, c_gate, g_a, g_b, g_c) = jnp.split(u @ w_in[l], IN_SPLITS, axis=-1)
        (a_in_x, a_gate_x, q_x, k_x, v_x, b_gate_x, c_in_x, c_gate_x, g_a_x, g_b_x, g_c_x) = jnp.split(
            ux @ w_in[l], IN_SPLITS, axis=-1)

        xa = dwconv_centred(a_in, lru_conv_w[l], lru_conv_b[l])
        xa_x = dwconv_centred(a_in_x, lru_conv_w[l], lru_conv_b[l])
        ya_x, ya = rglru_bidir(xa_x, xa, lru_w_a[l], lru_b_a[l], lru_w_x[l], lru_b_x[l], lru_lambda[l])

        kh_x, vh_x = _heads(k_x), _heads(v_x)
        yb = neighbourhood_attention(_heads(q), _heads(k), _heads(v), kh_x, vh_x, na_rpb[l])

        hf, hb = hyena_filters(seq_len, hy_w1[l], hy_b1[l], hy_freq1[l], hy_w2[l], hy_b2[l], hy_freq2[l], hy_w3[l])
        yc = hyena_mix(c_in, hy_conv_w[l], hy_conv_b[l], hf, hb, hy_d_skip[l])

        out = merge_branches(ya, yb, yc, a_gate, b_gate, c_gate, g_a, g_b, g_c, w_branch_out[l], w_out[l])
        h_new = layer_norm(ALPHA * h + gate * out, ln_g[l], ln_b[l])

        if not last:
            yb_x = context_attention(_heads(q_x), kh_x, vh_x)
            hf_x, hb_x = hyena_filters(ctx_len, hy_w1[l], hy_b1[l], hy_freq1[l], hy_w2[l], hy_b2[l],
                                       hy_freq2[l], hy_w3[l])
            yc_x = hyena_mix(c_in_x, hy_conv_w[l], hy_conv_b[l], hf_x, hb_x, hy_d_skip[l])
            out_x = merge_branches(ya_x, yb_x, yc_x, a_gate_x, b_gate_x, c_gate_x, g_a_x, g_b_x, g_c_x,
                                   w_branch_out[l], w_out[l])
            hc = layer_norm(ALPHA * hc + gate_x * out_x, ln_g[l], ln_b[l])
        h = h_new
    return h
```

```python
import functools
import math

import numpy as np
import jax
import jax.numpy as jnp
from jax import lax
from jax.experimental import pallas as pl
from jax.experimental.pallas import tpu as pltpu

F32 = jnp.float32
BF16 = jnp.bfloat16

D_MODEL = 1024
D_BRANCH = 512
N_LRU_BLOCKS = 8
LRU_CONV = 4
LRU_C = 8.0
NA_HEAD_DIM = 64
NA_HEADS = 8
NA_KH = 8
NA_KW = 16
GRID_W = 64
HY_CONV = 3
HY_POS_BANDS = 16
HY_EMB = 1 + 2 * HY_POS_BANDS
HY_HID = 64
HY_DECAY_TARGET = 1e-2
HY_FAST_DECAY = 0.3
HY_SLOW_DECAY = 1.5
LN_EPS = 1e-5
D_IN = 8192

COL_LRU_X, COL_LRU_G = 0, 512
COL_Q, COL_K, COL_V, COL_NA_G = 1024, 1536, 2048, 2560
COL_HY, COL_HY_G = 3072, 4608
COL_GA, COL_GB, COL_GC = 5120, 6144, 7168

LANES = 128
VMEM_LIMIT = 56 * 1024 * 1024
MASK_VALUE = -1e30

FFT_N1 = 128
FFT_N2 = 64
FFT_G = 8
FFT_KB = 8
FFT_KB_SPEC = 2
FFT_CT = 256


def _params(*sem):
    return pltpu.CompilerParams(dimension_semantics=sem, vmem_limit_bytes=VMEM_LIMIT)


def _split(x):
    hi = x.astype(BF16)
    lo = (x - hi.astype(F32)).astype(BF16)
    return hi, lo


def _bdot(a, b):
    return jnp.dot(a, b, preferred_element_type=F32)


def _dot3(a, b):
    ah, al = _split(a)
    bh, bl = _split(b)
    return _bdot(ah, bh) + _bdot(ah, bl) + _bdot(al, bh)


def _dot3_t(th, tl, dh, dl):
    return _bdot(th, dh) + _bdot(th, dl) + _bdot(tl, dh)


def _sigmoid(x):
    return 1.0 / (1.0 + jnp.exp(-x))


def _silu(x):
    return x * _sigmoid(x)


def _layer_norm(x):
    mu = jnp.mean(x, axis=-1, keepdims=True)
    xc = x - mu
    var = jnp.mean(xc * xc, axis=-1, keepdims=True)
    return xc * lax.rsqrt(var + LN_EPS)


def _mod_kernel(c_ref, w_ref, b_ref, o_ref):
    o_ref[...] = _dot3(_silu(c_ref[...]), w_ref[...]) + b_ref[...]


def _modulation(c_rows, w_mod, b_mod):
    rows, d = c_rows.shape
    n = w_mod.shape[1]
    tn = 512
    return pl.pallas_call(
        _mod_kernel,
        grid=(n // tn,),
        in_specs=[pl.BlockSpec((rows, d), lambda j: (0, 0)),
                  pl.BlockSpec((d, tn), lambda j: (0, j)),
                  pl.BlockSpec((1, tn), lambda j: (0, j))],
        out_specs=pl.BlockSpec((rows, tn), lambda j: (0, j)),
        out_shape=jax.ShapeDtypeStruct((rows, n), F32),
        compiler_params=_params("parallel"),
        name="modulation",
    )(c_rows, w_mod, b_mod.reshape(1, n))


def _proj_kernel(x_ref, sc_ref, sh_ref, w_ref, o_ref, u_scr):
    @pl.when(pl.program_id(2) == 0)
    def _():
        u = _layer_norm(x_ref[0]) * (1.0 + sc_ref[0]) + sh_ref[0]
        u_scr[...] = u.astype(BF16)

    o_ref[0] = _bdot(u_scr[...], w_ref[...])


def _in_projection(x, scale, shift, w_bf16):
    bsz, length, d = x.shape
    n = w_bf16.shape[1]
    tm = min(length, 1024)
    tn = 1024
    return pl.pallas_call(
        _proj_kernel,
        grid=(bsz, length // tm, n // tn),
        in_specs=[pl.BlockSpec((1, tm, d), lambda b, i, j: (b, i, 0)),
                  pl.BlockSpec((1, 1, d), lambda b, i, j: (b, 0, 0)),
                  pl.BlockSpec((1, 1, d), lambda b, i, j: (b, 0, 0)),
                  pl.BlockSpec((d, tn), lambda b, i, j: (0, j))],
        out_specs=pl.BlockSpec((1, tm, tn), lambda b, i, j: (b, i, j)),
        out_shape=jax.ShapeDtypeStruct((bsz, length, n), F32),
        scratch_shapes=[pltpu.VMEM((tm, d), BF16)],
        compiler_params=_params("parallel", "parallel", "arbitrary"),
        name="in_projection",
    )(x, scale, shift, w_bf16)


def _shift_rows(x, s):
    if s == 0:
        return x
    length = x.shape[0]
    y = pltpu.roll(x, (-s) % length, axis=0)
    rows = lax.broadcasted_iota(jnp.int32, x.shape, 0)
    keep = rows < length - s if s > 0 else rows >= -s
    return jnp.where(keep, y, 0.0)


def _dwconv(x, w_ref, b_ref, kw):
    left = kw // 2
    acc = x * w_ref[left:left + 1, :] + b_ref[...]
    for k in range(kw):
        if k != left:
            acc = acc + _shift_rows(x, k - left) * w_ref[k:k + 1, :]
    return acc


def _lru_conv_kernel(x_ref, w_ref, b_ref, o_ref):
    o_ref[0] = _dwconv(x_ref[0], w_ref, b_ref, LRU_CONV)


def _lru_conv(proj, w, b):
    bsz, length, _ = proj.shape
    nct = D_BRANCH // LANES
    c0 = COL_LRU_X // LANES
    return pl.pallas_call(
        _lru_conv_kernel,
        grid=(bsz, nct),
        in_specs=[pl.BlockSpec((1, length, LANES), lambda b, c: (b, 0, c0 + c)),
                  pl.BlockSpec((LRU_CONV, LANES), lambda b, c: (0, c)),
                  pl.BlockSpec((1, LANES), lambda b, c: (0, c))],
        out_specs=pl.BlockSpec((1, length, LANES), lambda b, c: (b, 0, c)),
        out_shape=jax.ShapeDtypeStruct((bsz, length, D_BRANCH), F32),
        compiler_params=_params("parallel", "parallel"),
        name="lru_conv",
    )(proj, w, b.reshape(1, -1))


def _lru_kernel(xf_ref, xb_ref, w_ref, ba_ref, bx_ref, sp_ref, h0_ref, of_ref, ob_ref,
                a0, a1, b0, b1, o0, o1, h_scr, *, tt, stride):
    nb = xf_ref.shape[0]
    ch = xf_ref.shape[2]
    nct = ch // LANES

    @pl.when(pl.program_id(0) == 0)
    def _():
        h_scr[...] = h0_ref[...]

    for d, x_ref, a_scr, b_scr in ((0, xf_ref, a0, b0), (1, xb_ref, a1, b1)):
        for b in range(nb):
            x = x_ref[b]
            pre = _bdot(x.astype(BF16), w_ref[d])
            r = _sigmoid(pre[:, :ch] + ba_ref[d])
            ig = _sigmoid(pre[:, ch:] + bx_ref[d])
            log_a = (-LRU_C) * r * sp_ref[d]
            a = jnp.exp(log_a)
            bb = jnp.sqrt(1.0 - jnp.exp(2.0 * log_a)) * (ig * x)
            for ci in range(nct):
                sl = slice(ci * LANES, (ci + 1) * LANES)
                a_scr[ci, pl.ds(b * stride, tt), :] = a[:, sl]
                b_scr[ci, pl.ds(b * stride, tt), :] = bb[:, sl]

    def step(t, carry):
        tb = tt - 1 - t
        new = []
        for ci in range(nct):
            hf = a0[ci, pl.ds(t, nb, stride=stride), :] * carry[ci] + b0[ci, pl.ds(t, nb, stride=stride), :]
            o0[ci, pl.ds(t, nb, stride=stride), :] = hf
            new.append(hf)
        for ci in range(nct):
            hb = a1[ci, pl.ds(tb, nb, stride=stride), :] * carry[nct + ci] + b1[ci, pl.ds(tb, nb, stride=stride), :]
            o1[ci, pl.ds(tb, nb, stride=stride), :] = hb
            new.append(hb)
        return tuple(new)

    init = tuple(h_scr[d, :, ci * LANES:(ci + 1) * LANES] for d in range(2) for ci in range(nct))
    fin = lax.fori_loop(0, tt, step, init)
    for d in range(2):
        for ci in range(nct):
            h_scr[d, :, ci * LANES:(ci + 1) * LANES] = fin[d * nct + ci]
    for b in range(nb):
        for ci in range(nct):
            sl = slice(ci * LANES, (ci + 1) * LANES)
            of_ref[b, :, sl] = o0[ci, pl.ds(b * stride, tt), :]
            ob_ref[b, :, sl] = o1[ci, pl.ds(b * stride, tt), :]


def _rglru(xa, w_cat, b_a, b_x, sp, h0):
    bsz, length, ch = xa.shape
    tt = 128
    stride = tt + 8
    nt = length // tt
    kern = functools.partial(_lru_kernel, tt=tt, stride=stride)
    scr = pltpu.VMEM((ch // LANES, bsz * stride, LANES), F32)
    return pl.pallas_call(
        kern,
        grid=(nt,),
        in_specs=[pl.BlockSpec((bsz, tt, ch), lambda i: (0, i, 0)),
                  pl.BlockSpec((bsz, tt, ch), lambda i: (0, nt - 1 - i, 0)),
                  pl.BlockSpec((2, ch, 2 * ch), lambda i: (0, 0, 0)),
                  pl.BlockSpec((2, 1, ch), lambda i: (0, 0, 0)),
                  pl.BlockSpec((2, 1, ch), lambda i: (0, 0, 0)),
                  pl.BlockSpec((2, 1, ch), lambda i: (0, 0, 0)),
                  pl.BlockSpec((2, bsz, ch), lambda i: (0, 0, 0))],
        out_specs=[pl.BlockSpec((bsz, tt, ch), lambda i: (0, i, 0)),
                   pl.BlockSpec((bsz, tt, ch), lambda i: (0, nt - 1 - i, 0))],
        out_shape=[jax.ShapeDtypeStruct((bsz, length, ch), F32)] * 2,
        scratch_shapes=[scr, scr, scr, scr, scr, scr, pltpu.VMEM((2, bsz, ch), F32)],
        compiler_params=_params("arbitrary"),
        name="rglru",
    )(xa, xa, w_cat, b_a, b_x, sp, h0)


def _softmax_pv(q, k_loc, v_loc, bias, k_ctx, v_ctx):
    dn = (((1,), (1,)), ((), ()))
    s_loc = lax.dot_general(q, k_loc, dn, preferred_element_type=F32)
    if bias is not None:
        s_loc = s_loc + bias
    s_ctx = lax.dot_general(q, k_ctx, dn, preferred_element_type=F32)
    m = jnp.maximum(jnp.max(s_loc, axis=-1, keepdims=True), jnp.max(s_ctx, axis=-1, keepdims=True))
    e_loc = jnp.exp(s_loc - m)
    e_ctx = jnp.exp(s_ctx - m)
    inv = 1.0 / (jnp.sum(e_loc, axis=-1, keepdims=True) + jnp.sum(e_ctx, axis=-1, keepdims=True))
    return (_bdot((e_loc * inv).astype(BF16), v_loc) + _bdot((e_ctx * inv).astype(BF16), v_ctx))


def _na_kernel(q_ref, k_ref, v_ref, kc_ref, vc_ref, bias_ref, o_ref, *, rows):
    r = pl.program_id(2)
    kh = min(NA_KH, rows)
    r0 = jnp.clip(r - kh // 2, 0, rows - kh)
    start = pl.multiple_of(r0 * GRID_W, GRID_W)
    k_band = k_ref[0, pl.ds(start, kh * GRID_W), :].astype(BF16)
    v_band = v_ref[0, pl.ds(start, kh * GRID_W), :].astype(BF16)
    q = (q_ref[0] * NA_HEAD_DIM ** -0.5).astype(BF16)
    kc = kc_ref[0].astype(BF16)
    vc = vc_ref[0].astype(BF16)
    outs = []
    for h in range(LANES // NA_HEAD_DIM):
        sl = slice(h * NA_HEAD_DIM, (h + 1) * NA_HEAD_DIM)
        outs.append(_softmax_pv(q[:, sl], k_band[:, sl], v_band[:, sl], bias_ref[0, h], kc[:, sl], vc[:, sl]))
    o_ref[0] = jnp.concatenate(outs, axis=-1)


def _na_bias(rpb):
    cols = np.arange(GRID_W)
    col_start = np.clip(cols - NA_KW // 2, 0, GRID_W - NA_KW)
    col_ok = (cols[None, :] >= col_start[:, None]) & (cols[None, :] < col_start[:, None] + NA_KW)
    dc_idx = np.clip(cols[None, :] - cols[:, None], -(NA_KW - 1), NA_KW - 1) + (NA_KW - 1)
    tables = []
    for off in range(NA_KH):
        dr_idx = np.arange(NA_KH) - off + (NA_KH - 1)
        bias = rpb[:, dr_idx[None, :, None], dc_idx[:, None, :]]
        bias = jnp.where(col_ok[None, :, None, :], bias, MASK_VALUE)
        tables.append(bias.reshape(NA_HEADS, GRID_W, NA_KH * GRID_W))
    return jnp.stack(tables, 0).astype(F32)


def _neighbourhood_attention(proj, proj_ctx, bias):
    bsz, length, _ = proj.shape
    lc = proj_ctx.shape[1]
    rows = length // GRID_W
    kh = min(NA_KH, rows)
    nhp = D_BRANCH // LANES
    hpb = LANES // NA_HEAD_DIM
    cq, ck, cv = COL_Q // LANES, COL_K // LANES, COL_V // LANES

    def bias_map(b, hp, r):
        return (r - jnp.clip(r - kh // 2, 0, rows - kh), hp, 0, 0)

    return pl.pallas_call(
        functools.partial(_na_kernel, rows=rows),
        grid=(bsz, nhp, rows),
        in_specs=[pl.BlockSpec((1, GRID_W, LANES), lambda b, hp, r: (b, r, cq + hp)),
                  pl.BlockSpec((1, length, LANES), lambda b, hp, r: (b, 0, ck + hp)),
                  pl.BlockSpec((1, length, LANES), lambda b, hp, r: (b, 0, cv + hp)),
                  pl.BlockSpec((1, lc, LANES), lambda b, hp, r: (b, 0, ck + hp)),
                  pl.BlockSpec((1, lc, LANES), lambda b, hp, r: (b, 0, cv + hp)),
                  pl.BlockSpec((1, hpb, GRID_W, kh * GRID_W), bias_map)],
        out_specs=pl.BlockSpec((1, GRID_W, LANES), lambda b, hp, r: (b, r, hp)),
        out_shape=jax.ShapeDtypeStruct((bsz, length, D_BRANCH), F32),
        compiler_params=_params("parallel", "parallel", "arbitrary"),
        name="neighbourhood_attention",
    )(proj, proj, proj, proj_ctx, proj_ctx, bias)


def _ctx_attn_kernel(q_ref, k_ref, v_ref, o_ref):
    q = (q_ref[0] * NA_HEAD_DIM ** -0.5).astype(BF16)
    k = k_ref[0].astype(BF16)
    v = v_ref[0].astype(BF16)
    dn = (((1,), (1,)), ((), ()))
    outs = []
    for h in range(LANES // NA_HEAD_DIM):
        sl = slice(h * NA_HEAD_DIM, (h + 1) * NA_HEAD_DIM)
        s = lax.dot_general(q[:, sl], k[:, sl], dn, preferred_element_type=F32)
        e = jnp.exp(s - jnp.max(s, axis=-1, keepdims=True))
        p = e * (1.0 / jnp.sum(e, axis=-1, keepdims=True))
        outs.append(_bdot(p.astype(BF16), v[:, sl]))
    o_ref[0] = jnp.concatenate(outs, axis=-1)


def _context_attention(proj_ctx):
    bsz, lc, _ = proj_ctx.shape
    nhp = D_BRANCH // LANES
    cq, ck, cv = COL_Q // LANES, COL_K // LANES, COL_V // LANES
    return pl.pallas_call(
        _ctx_attn_kernel,
        grid=(bsz, nhp),
        in_specs=[pl.BlockSpec((1, lc, LANES), lambda b, hp: (b, 0, cq + hp)),
                  pl.BlockSpec((1, lc, LANES), lambda b, hp: (b, 0, ck + hp)),
                  pl.BlockSpec((1, lc, LANES), lambda b, hp: (b, 0, cv + hp))],
        out_specs=pl.BlockSpec((1, lc, LANES), lambda b, hp: (b, 0, hp)),
        out_shape=jax.ShapeDtypeStruct((bsz, lc, D_BRANCH), F32),
        compiler_params=_params("parallel", "parallel"),
        name="context_attention",
    )(proj_ctx, proj_ctx, proj_ctx)


def _hy_pre_kernel(x0_ref, x1_ref, v_ref, w_ref, b_ref, x0_out, w_out):
    x0_out[0] = _dwconv(x0_ref[0], w_ref.at[0], b_ref.at[0], HY_CONV)
    x1 = _dwconv(x1_ref[0], w_ref.at[1], b_ref.at[1], HY_CONV)
    v = _dwconv(v_ref[0], w_ref.at[2], b_ref.at[2], HY_CONV)
    w_out[0] = x1 * v


def _hyena_pre(proj, conv_w, conv_b):
    bsz, length, _ = proj.shape
    nct = D_BRANCH // LANES
    c0 = COL_HY // LANES
    w3 = conv_w.reshape(HY_CONV, 3, D_BRANCH).transpose(1, 0, 2)
    b3 = conv_b.reshape(3, 1, D_BRANCH)
    spec = lambda seg: pl.BlockSpec((1, length, LANES), lambda b, c: (b, 0, c0 + seg * nct + c))
    return pl.pallas_call(
        _hy_pre_kernel,
        grid=(bsz, nct),
        in_specs=[spec(0), spec(1), spec(2),
                  pl.BlockSpec((3, HY_CONV, LANES), lambda b, c: (0, 0, c)),
                  pl.BlockSpec((3, 1, LANES), lambda b, c: (0, 0, c))],
        out_specs=[pl.BlockSpec((1, length, LANES), lambda b, c: (b, 0, c))] * 2,
        out_shape=[jax.ShapeDtypeStruct((bsz, length, D_BRANCH), F32)] * 2,
        compiler_params=_params("parallel", "parallel"),
        name="hyena_pre",
    )(proj, proj, proj, w3, b3)


def _filter_kernel(z_ref, w1_ref, b1_ref, f1_ref, w2_ref, b2_ref, f2_ref, w3_ref, dl_ref, hf_ref, hb_ref):
    z = z_ref[...]
    hid = jnp.sin(f1_ref[...] * (_dot3(z, w1_ref[...]) + b1_ref[...]))
    hid = jnp.sin(f2_ref[...] * (_dot3(hid, w2_ref[...]) + b2_ref[...]))
    filt = _dot3(hid, w3_ref[...])
    decay = jnp.exp(-z[:, 0:1] * dl_ref[...])
    hf_ref[...] = filt[:, :D_BRANCH] * decay
    hb_ref[...] = filt[:, D_BRANCH:] * decay


def _hyena_filters(length, w1, b1, freq1, w2, b2, freq2, w3):
    t = jnp.arange(length, dtype=F32) / length
    ang = 2.0 * math.pi * t[:, None] * jnp.arange(1, HY_POS_BANDS + 1, dtype=F32)
    z = jnp.concatenate([t[:, None], jnp.cos(ang), jnp.sin(ang)], -1)
    z = jnp.pad(z, ((0, 0), (0, LANES - HY_EMB)))
    w1p = jnp.pad(w1, ((0, LANES - HY_EMB), (0, 0)))
    deltas = jnp.abs(jnp.linspace(math.log(HY_DECAY_TARGET) / HY_FAST_DECAY,
                                  math.log(HY_DECAY_TARGET) / HY_SLOW_DECAY, D_BRANCH, dtype=F32))
    tl = min(length, 512)
    full = lambda a: pl.BlockSpec(a.shape, lambda i: (0,) * a.ndim)
    args = [w1p, b1.reshape(1, -1), freq1.reshape(1, -1), w2, b2.reshape(1, -1), freq2.reshape(1, -1),
            w3, deltas.reshape(1, -1)]
    return pl.pallas_call(
        _filter_kernel,
        grid=(length // tl,),
        in_specs=[pl.BlockSpec((tl, LANES), lambda i: (i, 0))] + [full(a) for a in args],
        out_specs=[pl.BlockSpec((tl, D_BRANCH), lambda i: (i, 0))] * 2,
        out_shape=[jax.ShapeDtypeStruct((length, D_BRANCH), F32)] * 2,
        compiler_params=_params("parallel"),
        name="hyena_filters",
    )(z, *args)


def _fft_tables():
    n = FFT_N1 * FFT_N2
    half = FFT_N1 // 2
    k1 = np.arange(FFT_N1, dtype=np.float64)[:, None]
    n1 = np.arange(half, dtype=np.float64)[None, :]
    t1 = np.zeros((FFT_N2, 2 * FFT_N1, 2 * half))
    for n2 in range(FFT_N2):
        th = 2.0 * np.pi * k1 * (FFT_N2 * n1 + n2) / n
        gc, gs = np.cos(th), np.sin(th)
        t1[n2] = np.block([[gc, gs], [-gs, gc]])
    t1_inv = np.transpose(t1, (0, 2, 1)) / n
    k2 = np.arange(FFT_N2, dtype=np.float64)
    th2 = 2.0 * np.pi * np.outer(k2, k2) / FFT_N2
    fc, fs = np.cos(th2), np.sin(th2)
    t2 = np.block([[fc, fs], [-fs, fc]])
    t2_inv = np.block([[fc, -fs], [fs, fc]])

    def hl(a):
        a32 = jnp.asarray(a, F32)
        hi = a32.astype(BF16)
        return hi, (a32 - hi.astype(F32)).astype(BF16)

    return hl(t1), hl(t1_inv), hl(t2), hl(t2_inv)


def _fft1_kernel(z_ref, th_ref, tl_ref, a_ref):
    cw = z_ref.shape[2] // FFT_G
    for j in range(FFT_G):
        for c in range(cw // FFT_CT):
            sl = slice(j * cw + c * FFT_CT, j * cw + (c + 1) * FFT_CT)
            d = jnp.concatenate([z_ref[0, :, sl], z_ref[1, :, sl]], axis=0)
            dh, dl = _split(d)
            a_ref[0, :, sl] = _dot3_t(th_ref[j], tl_ref[j], dh, dl)


def _fft_stage1(w, t1):
    bsz, length, ch = w.shape
    half = FFT_N1 // 2
    assert length == half * FFT_N2 and bsz % 2 == 0
    zv = w.reshape(bsz, half, FFT_N2 * ch)
    blk = FFT_G * ch
    tspec = pl.BlockSpec((FFT_G, 2 * FFT_N1, 2 * half), lambda p, g: (g, 0, 0))
    return pl.pallas_call(
        _fft1_kernel,
        grid=(bsz // 2, FFT_N2 // FFT_G),
        in_specs=[pl.BlockSpec((2, half, blk), lambda p, g: (p, 0, g)), tspec, tspec],
        out_specs=pl.BlockSpec((1, 2 * FFT_N1, blk), lambda p, g: (p, 0, g)),
        out_shape=jax.ShapeDtypeStruct((bsz // 2, 2 * FFT_N1, FFT_N2 * ch), F32),
        compiler_params=_params("parallel", "parallel"),
        name="fft_stage1",
    )(zv, *t1)


def _fft2_spec_kernel(a_ref, th_ref, tl_ref, k_ref):
    ch = a_ref.shape[4]
    for k in range(FFT_KB_SPEC):
        sign = 1.0 if k % 2 == 0 else -1.0
        for c in range(ch // FFT_CT):
            sl = slice(c * FFT_CT, (c + 1) * FFT_CT)
            acc = None
            for p in range(2):
                d = jnp.concatenate([a_ref[p, 0, k, :, sl], a_ref[p, 1, k, :, sl]], axis=0)
                dh, dl = _split(d)
                x = _dot3_t(th_ref[...], tl_ref[...], dh, dl)
                acc = x if p == 0 else acc + sign * x
            k_ref[k, :, sl] = acc


def _fft_spectrum(a_filt, t2):
    ch = a_filt.shape[2] // FFT_N2
    av = a_filt.reshape(2, 2, FFT_N1, FFT_N2, ch)
    tspec = pl.BlockSpec((2 * FFT_N2, 2 * FFT_N2), lambda i: (0, 0))
    return pl.pallas_call(
        _fft2_spec_kernel,
        grid=(FFT_N1 // FFT_KB_SPEC,),
        in_specs=[pl.BlockSpec((2, 2, FFT_KB_SPEC, FFT_N2, ch), lambda i: (0, 0, i, 0, 0)), tspec, tspec],
        out_specs=pl.BlockSpec((FFT_KB_SPEC, 2 * FFT_N2, ch), lambda i: (i, 0, 0)),
        out_shape=jax.ShapeDtypeStruct((FFT_N1, 2 * FFT_N2, ch), F32),
        compiler_params=_params("parallel"),
        name="fft_spectrum",
    )(av, *t2)


def _fft2_kernel(a_ref, k_ref, th_ref, tl_ref, ih_ref, il_ref, b_ref):
    ch = a_ref.shape[4]
    for k in range(FFT_KB):
        for c in range(ch // FFT_CT):
            sl = slice(c * FFT_CT, (c + 1) * FFT_CT)
            d = jnp.concatenate([a_ref[0, 0, k, :, sl], a_ref[0, 1, k, :, sl]], axis=0)
            dh, dl = _split(d)
            x = _dot3_t(th_ref[...], tl_ref[...], dh, dl)
            xr, xi = x[:FFT_N2], x[FFT_N2:]
            kr, ki = k_ref[k, :FFT_N2, sl], k_ref[k, FFT_N2:, sl]
            y = jnp.concatenate([xr * kr - xi * ki, xr * ki + xi * kr], axis=0)
            yh, yl = _split(y)
            out = _dot3_t(ih_ref[...], il_ref[...], yh, yl)
            b_ref[0, 0, k, :, sl] = out[:FFT_N2]
            b_ref[0, 1, k, :, sl] = out[FFT_N2:]


def _fft_stage2(a, kf, t2, t2_inv):
    npair = a.shape[0]
    ch = a.shape[2] // FFT_N2
    av = a.reshape(npair, 2, FFT_N1, FFT_N2, ch)
    tspec = pl.BlockSpec((2 * FFT_N2, 2 * FFT_N2), lambda i, p: (0, 0))
    blk = pl.BlockSpec((1, 2, FFT_KB, FFT_N2, ch), lambda i, p: (p, 0, i, 0, 0))
    out = pl.pallas_call(
        _fft2_kernel,
        grid=(FFT_N1 // FFT_KB, npair),
        in_specs=[blk, pl.BlockSpec((FFT_KB, 2 * FFT_N2, ch), lambda i, p: (i, 0, 0)),
                  tspec, tspec, tspec, tspec],
        out_specs=blk,
        out_shape=jax.ShapeDtypeStruct(av.shape, F32),
        compiler_params=_params("parallel", "parallel"),
        name="fft_stage2",
    )(av, kf, *t2, *t2_inv)
    return out.reshape(npair, 2 * FFT_N1, FFT_N2 * ch)


def _fft3_kernel(b_ref, th_ref, tl_ref, y_ref):
    half = FFT_N1 // 2
    cw = b_ref.shape[2] // FFT_G
    for j in range(FFT_G):
        for c in range(cw // FFT_CT):
            sl = slice(j * cw + c * FFT_CT, j * cw + (c + 1) * FFT_CT)
            dh, dl = _split(b_ref[0, :, sl])
            y = _dot3_t(th_ref[j], tl_ref[j], dh, dl)
            y_ref[0, :, sl] = y[:half]
            y_ref[1, :, sl] = y[half:]


def _fft_stage3(bm, t1_inv, bsz, length, ch):
    half = FFT_N1 // 2
    blk = FFT_G * ch
    tspec = pl.BlockSpec((FFT_G, 2 * half, 2 * FFT_N1), lambda p, g: (g, 0, 0))
    y = pl.pallas_call(
        _fft3_kernel,
        grid=(bsz // 2, FFT_N2 // FFT_G),
        in_specs=[pl.BlockSpec((1, 2 * FFT_N1, blk), lambda p, g: (p, 0, g)), tspec, tspec],
        out_specs=pl.BlockSpec((2, half, blk), lambda p, g: (p, 0, g)),
        out_shape=jax.ShapeDtypeStruct((bsz, half, FFT_N2 * ch), F32),
        compiler_params=_params("parallel", "parallel"),
        name="fft_stage3",
    )(bm, *t1_inv)
    return y.reshape(bsz, length, ch)


def _hyena_long_conv(w, h_fwd, h_bwd, tables):
    bsz, length, ch = w.shape
    t1, t1_inv, t2, t2_inv = tables
    zeros = jnp.zeros_like(h_fwd)
    g = jnp.concatenate([zeros[:1], jnp.flip(h_bwd[1:], 0)], 0)
    filt = jnp.stack([h_fwd, zeros, g, zeros], 0)
    kf = _fft_spectrum(_fft_stage1(filt, t1), t2)
    a = _fft_stage1(w, t1)
    bm = _fft_stage2(a, kf, t2, t2_inv)
    return _fft_stage3(bm, t1_inv, bsz, length, ch)


def _dense_dft_tables(length):
    n = 2 * length
    k = np.arange(n, dtype=np.float64)
    th = 2.0 * np.pi * np.outer(k, k) / n
    c, s = np.cos(th), np.sin(th)
    t_full = np.concatenate([c, -s], 0)
    t_fwd = t_full[:, :length]
    t_inv = np.concatenate([c[:length], -s[:length]], 1) / n
    return tuple(jnp.asarray(a, F32) for a in (t_full, t_fwd, t_inv))


def _ctx_spec_kernel(t_ref, h_ref, o_ref):
    o_ref[...] = _dot3(t_ref[...], h_ref[...])


def _ctx_conv_kernel(w_ref, k_ref, tf_ref, ti_ref, o_ref):
    n = k_ref.shape[0] // 2
    x = _dot3(tf_ref[...], w_ref[0])
    xr, xi = x[:n], x[n:]
    kr, ki = k_ref[:n], k_ref[n:]
    y = jnp.concatenate([xr * kr - xi * ki, xr * ki + xi * kr], axis=0)
    o_ref[0] = _dot3(ti_ref[...], y)


def _ctx_long_conv(w, h_fwd, h_bwd):
    bsz, length, ch = w.shape
    n = 2 * length
    t_full, t_fwd, t_inv = _dense_dft_tables(length)
    kern = jnp.concatenate([h_fwd, jnp.zeros_like(h_fwd[:1]), jnp.flip(h_bwd[1:], 0)], 0)
    kf = pl.pallas_call(
        _ctx_spec_kernel,
        grid=(1,),
        in_specs=[pl.BlockSpec((2 * n, n), lambda i: (0, 0)), pl.BlockSpec((n, ch), lambda i: (0, 0))],
        out_specs=pl.BlockSpec((2 * n, ch), lambda i: (0, 0)),
        out_shape=jax.ShapeDtypeStruct((2 * n, ch), F32),
        compiler_params=_params("arbitrary"),
        name="ctx_spectrum",
    )(t_full, kern)
    return pl.pallas_call(
        _ctx_conv_kernel,
        grid=(bsz,),
        in_specs=[pl.BlockSpec((1, length, ch), lambda b: (b, 0, 0)),
                  pl.BlockSpec((2 * n, ch), lambda b: (0, 0)),
                  pl.BlockSpec((2 * n, length), lambda b: (0, 0)),
                  pl.BlockSpec((length, 2 * n), lambda b: (0, 0))],
        out_specs=pl.BlockSpec((1, length, ch), lambda b: (b, 0, 0)),
        out_shape=jax.ShapeDtypeStruct((bsz, length, ch), F32),
        compiler_params=_params("parallel"),
        name="ctx_long_conv",
    )(w, kf, t_fwd, t_inv)


def _merge_kernel(hf_ref, hb_ref, yb_ref, cv_ref, w_ref, x0_ref, ag_ref, bg_ref, cg_ref,
                  ga_ref, gb_ref, gc_ref, h_ref, gate_ref, wb_ref, wo_ref, ds_ref, lg_ref, lb_ref,
                  o_ref, *, alpha):
    ya = hf_ref[0] + hb_ref[0]
    w = w_ref[0]
    yc = x0_ref[0] * (cv_ref[0] + ds_ref[...] * w)
    pa = _bdot((ya * _silu(ag_ref[0])).astype(BF16), wb_ref[0])
    pb = _bdot((yb_ref[0] * _silu(bg_ref[0])).astype(BF16), wb_ref[1])
    pc = _bdot((yc * _silu(cg_ref[0])).astype(BF16), wb_ref[2])
    m = _sigmoid(ga_ref[0]) * pa + _sigmoid(gb_ref[0]) * pb + _sigmoid(gc_ref[0]) * pc
    out = _bdot(m.astype(BF16), wo_ref[...])
    z = alpha * h_ref[0] + gate_ref[0] * out
    o_ref[0] = _layer_norm(z) * lg_ref[...] + lb_ref[...]


def _merge(proj, hf, hb, yb, conv, w, x0c, h, gate, wb_bf16, wo_bf16, d_skip, ln_g, ln_b, alpha):
    bsz, length, d = h.shape
    tm = 256
    half = lambda: pl.BlockSpec((1, tm, D_BRANCH), lambda b, i: (b, i, 0))
    pcol = lambda col, width: pl.BlockSpec((1, tm, width), lambda b, i: (b, i, col // width))
    row = lambda n: pl.BlockSpec((1, n), lambda b, i: (0, 0))
    return pl.pallas_call(
        functools.partial(_merge_kernel, alpha=alpha),
        grid=(bsz, length // tm),
        in_specs=[half(), half(), half(), half(), half(), half(),
                  pcol(COL_LRU_G, D_BRANCH), pcol(COL_NA_G, D_BRANCH), pcol(COL_HY_G, D_BRANCH),
                  pcol(COL_GA, D_MODEL), pcol(COL_GB, D_MODEL), pcol(COL_GC, D_MODEL),
                  pl.BlockSpec((1, tm, d), lambda b, i: (b, i, 0)),
                  pl.BlockSpec((1, 1, d), lambda b, i: (b, 0, 0)),
                  pl.BlockSpec((3, D_BRANCH, d), lambda b, i: (0, 0, 0)),
                  pl.BlockSpec((d, d), lambda b, i: (0, 0)),
                  row(D_BRANCH), row(d), row(d)],
        out_specs=pl.BlockSpec((1, tm, d), lambda b, i: (b, i, 0)),
        out_shape=jax.ShapeDtypeStruct((bsz, length, d), F32),
        compiler_params=_params("parallel", "parallel"),
        name="merge",
    )(hf, hb, yb, conv, w, x0c, proj, proj, proj, proj, proj, proj, h, gate,
      wb_bf16, wo_bf16, d_skip.reshape(1, -1), ln_g.reshape(1, -1), ln_b.reshape(1, -1))


def _block_diag(w):
    g, n, _ = w.shape
    eye = jnp.eye(g, dtype=w.dtype)
    return (eye[:, None, :, None] * w[:, :, None, :]).reshape(g * n, g * n)


def kernel(x, c, ctx, c_ctx, w_mod, b_mod, w_in, lru_conv_w, lru_conv_b, lru_w_a, lru_b_a, lru_w_x, lru_b_x,
           lru_lambda, na_rpb, hy_conv_w, hy_conv_b, hy_w1, hy_b1, hy_freq1, hy_w2, hy_b2, hy_freq2, hy_w3,
           hy_d_skip, w_branch_out, w_out, ln_g, ln_b):
    depth = w_in.shape[0]
    bsz, seq_len, d = x.shape
    ctx_len = ctx.shape[1]
    alpha = (2 * depth) ** 0.25
    tables = _fft_tables()
    pad_rows = (-(bsz + 1)) % 8
    c_rows = jnp.concatenate([c, c_ctx[None, :], jnp.zeros((pad_rows, d), F32)], 0)

    h, hc = x, ctx
    for l in range(depth):
        last = l == depth - 1
        mod = _modulation(c_rows, w_mod[l], b_mod[l])
        shift, scale, gate = (mod[:bsz, None, i * d:(i + 1) * d] for i in range(3))
        shift_x, scale_x, gate_x = (jnp.broadcast_to(mod[bsz, i * d:(i + 1) * d], (bsz, 1, d)) for i in range(3))

        w_in_l = w_in[l].astype(BF16)
        proj = _in_projection(h, scale, shift, w_in_l)
        proj_x = _in_projection(hc, scale_x, shift_x, w_in_l)

        w_cat = jnp.stack([jnp.concatenate([_block_diag(lru_w_a[l, dd]), _block_diag(lru_w_x[l, dd])], 1)
                           for dd in range(2)], 0).astype(BF16)
        b_a = lru_b_a[l][:, None, :]
        b_x = lru_b_x[l][:, None, :]
        sp = jax.nn.softplus(-lru_lambda[l])[:, None, :]
        xa = _lru_conv(proj, lru_conv_w[l], lru_conv_b[l])
        xa_x = _lru_conv(proj_x, lru_conv_w[l], lru_conv_b[l])
        hf_x, hb_x = _rglru(xa_x, w_cat, b_a, b_x, sp, jnp.zeros((2, bsz, D_BRANCH), F32))
        h0 = jnp.stack([hf_x[:, -1], hb_x[:, 0]], 0)
        hf, hb = _rglru(xa, w_cat, b_a, b_x, sp, h0)

        yb = _neighbourhood_attention(proj, proj_x, _na_bias(na_rpb[l]))

        filt_args = (hy_w1[l], hy_b1[l], hy_freq1[l], hy_w2[l], hy_b2[l], hy_freq2[l], hy_w3[l])
        x0c, w = _hyena_pre(proj, hy_conv_w[l], hy_conv_b[l])
        h_fwd, h_bwd = _hyena_filters(seq_len, *filt_args)
        conv = _hyena_long_conv(w, h_fwd, h_bwd, tables)

        wb = w_branch_out[l].astype(BF16)
        wo = w_out[l].astype(BF16)
        h_new = _merge(proj, hf, hb, yb, conv, w, x0c, h, gate, wb, wo, hy_d_skip[l], ln_g[l], ln_b[l], alpha)

        if not last:
            yb_x = _context_attention(proj_x)
            x0c_x, w_x = _hyena_pre(proj_x, hy_conv_w[l], hy_conv_b[l])
            hx_fwd, hx_bwd = _hyena_filters(ctx_len, *filt_args)
            conv_x = _ctx_long_conv(w_x, hx_fwd, hx_bwd)
            hc = _merge(proj_x, hf_x, hb_x, yb_x, conv_x, w_x, x0c_x, hc, gate_x, wb, wo, hy_d_skip[l],
                        ln_g[l], ln_b[l], alpha)
        h = h_new
    return h
```

```python
import functools
import math

import numpy as np
import jax
import jax.numpy as jnp
from jax import lax
from jax.experimental import pallas as pl
from jax.experimental.pallas import tpu as pltpu

F32 = jnp.float32
BF16 = jnp.bfloat16

D_MODEL = 1024
D_BRANCH = 512
N_LRU_BLOCKS = 8
LRU_CONV = 4
LRU_C = 8.0
NA_HEAD_DIM = 64
NA_HEADS = 8
NA_KH = 8
NA_KW = 16
NA_RPB = 8
GRID_W = 64
HY_CONV = 3
HY_POS_BANDS = 16
HY_EMB = 1 + 2 * HY_POS_BANDS
HY_HID = 64
HY_DECAY_TARGET = 1e-2
HY_FAST_DECAY = 0.3
HY_SLOW_DECAY = 1.5
LN_EPS = 1e-5
D_IN = 8192

COL_LRU_X, COL_LRU_G = 0, 512
COL_Q, COL_K, COL_V, COL_NA_G = 1024, 1536, 2048, 2560
COL_HY, COL_HY_G = 3072, 4608
COL_GA, COL_GB, COL_GC = 5120, 6144, 7168

LANES = 128
VMEM_LIMIT = 56 * 1024 * 1024
MASK_VALUE = -1e30

FFT_N1 = 128
FFT_N2 = 64
FFT_G = 8
FFT_KB = 8
FFT_KB_SPEC = 2
FFT_CT = 256


def _params(*sem):
    return pltpu.CompilerParams(dimension_semantics=sem, vmem_limit_bytes=VMEM_LIMIT)


def _split(x):
    hi = x.astype(BF16)
    lo = (x - hi.astype(F32)).astype(BF16)
    return hi, lo


def _bdot(a, b):
    return jnp.dot(a, b, preferred_element_type=F32)


def _dot3(a, b):
    ah, al = _split(a)
    bh, bl = _split(b)
    return _bdot(ah, bh) + _bdot(ah, bl) + _bdot(al, bh)


def _dot3_t(th, tl, dh, dl):
    return _bdot(th, dh) + _bdot(th, dl) + _bdot(tl, dh)


def _sigmoid(x):
    return 1.0 / (1.0 + jnp.exp(-x))


def _silu(x):
    return x * _sigmoid(x)


def _layer_norm(x):
    mu = jnp.mean(x, axis=-1, keepdims=True)
    xc = x - mu
    var = jnp.mean(xc * xc, axis=-1, keepdims=True)
    return xc * lax.rsqrt(var + LN_EPS)


def _mod_kernel(c_ref, w_ref, b_ref, o_ref):
    o_ref[...] = _dot3(_silu(c_ref[...]), w_ref[...]) + b_ref[...]


def _modulation(c_rows, w_mod, b_mod):
    rows, d = c_rows.shape
    n = w_mod.shape[1]
    tn = 512
    return pl.pallas_call(
        _mod_kernel,
        grid=(n // tn,),
        in_specs=[pl.BlockSpec((rows, d), lambda j: (0, 0)),
                  pl.BlockSpec((d, tn), lambda j: (0, j)),
                  pl.BlockSpec((1, tn), lambda j: (0, j))],
        out_specs=pl.BlockSpec((rows, tn), lambda j: (0, j)),
        out_shape=jax.ShapeDtypeStruct((rows, n), F32),
        compiler_params=_params("parallel"),
        name="modulation",
    )(c_rows, w_mod, b_mod.reshape(1, n))


def _proj_kernel(x_ref, sc_ref, sh_ref, w_ref, o_ref, u_scr):
    @pl.when(pl.program_id(2) == 0)
    def _():
        u = _layer_norm(x_ref[0]) * (1.0 + sc_ref[0]) + sh_ref[0]
        u_scr[...] = u.astype(BF16)

    o_ref[0] = _bdot(u_scr[...], w_ref[...])


def _in_projection(x, scale, shift, w_bf16):
    bsz, length, d = x.shape
    n = w_bf16.shape[1]
    tm = min(length, 1024)
    tn = 1024
    return pl.pallas_call(
        _proj_kernel,
        grid=(bsz, length // tm, n // tn),
        in_specs=[pl.BlockSpec((1, tm, d), lambda b, i, j: (b, i, 0)),
                  pl.BlockSpec((1, 1, d), lambda b, i, j: (b, 0, 0)),
                  pl.BlockSpec((1, 1, d), lambda b, i, j: (b, 0, 0)),
                  pl.BlockSpec((d, tn), lambda b, i, j: (0, j))],
        out_specs=pl.BlockSpec((1, tm, tn), lambda b, i, j: (b, i, j)),
        out_shape=jax.ShapeDtypeStruct((bsz, length, n), F32),
        scratch_shapes=[pltpu.VMEM((tm, d), BF16)],
        compiler_params=_params("parallel", "parallel", "arbitrary"),
        name="in_projection",
    )(x, scale, shift, w_bf16)


def _shift_rows(x, s):
    if s == 0:
        return x
    length = x.shape[0]
    y = pltpu.roll(x, (-s) % length, axis=0)
    rows = lax.broadcasted_iota(jnp.int32, x.shape, 0)
    keep = rows < length - s if s > 0 else rows >= -s
    return jnp.where(keep, y, 0.0)


def _dwconv(x, w_ref, b_ref, kw):
    left = kw // 2
    acc = x * w_ref[left:left + 1, :] + b_ref[...]
    for k in range(kw):
        if k != left:
            acc = acc + _shift_rows(x, k - left) * w_ref[k:k + 1, :]
    return acc


def _lru_conv_kernel(x_ref, w_ref, b_ref, o_ref):
    o_ref[0] = _dwconv(x_ref[0], w_ref, b_ref, LRU_CONV)


def _lru_conv(proj, w, b):
    bsz, length, _ = proj.shape
    nct = D_BRANCH // LANES
    c0 = COL_LRU_X // LANES
    return pl.pallas_call(
        _lru_conv_kernel,
        grid=(bsz, nct),
        in_specs=[pl.BlockSpec((1, length, LANES), lambda b, c: (b, 0, c0 + c)),
                  pl.BlockSpec((LRU_CONV, LANES), lambda b, c: (0, c)),
                  pl.BlockSpec((1, LANES), lambda b, c: (0, c))],
        out_specs=pl.BlockSpec((1, length, LANES), lambda b, c: (b, 0, c)),
        out_shape=jax.ShapeDtypeStruct((bsz, length, D_BRANCH), F32),
        compiler_params=_params("parallel", "parallel"),
        name="lru_conv",
    )(proj, w, b.reshape(1, -1))


def _lru_kernel(xf_ref, xb_ref, w_ref, ba_ref, bx_ref, sp_ref, h0_ref, of_ref, ob_ref,
                a0, a1, b0, b1, o0, o1, h_scr, *, tt, stride):
    nb = xf_ref.shape[0]
    ch = xf_ref.shape[2]
    nct = ch // LANES

    @pl.when(pl.program_id(0) == 0)
    def _():
        h_scr[...] = h0_ref[...]

    for d, x_ref, a_scr, b_scr in ((0, xf_ref, a0, b0), (1, xb_ref, a1, b1)):
        for b in range(nb):
            x = x_ref[b]
            pre = _bdot(x.astype(BF16), w_ref[d])
            r = _sigmoid(pre[:, :ch] + ba_ref[d])
            ig = _sigmoid(pre[:, ch:] + bx_ref[d])
            log_a = (-LRU_C) * r * sp_ref[d]
            a = jnp.exp(log_a)
            bb = jnp.sqrt(1.0 - jnp.exp(2.0 * log_a)) * (ig * x)
            for ci in range(nct):
                sl = slice(ci * LANES, (ci + 1) * LANES)
                a_scr[ci, pl.ds(b * stride, tt), :] = a[:, sl]
                b_scr[ci, pl.ds(b * stride, tt), :] = bb[:, sl]

    def step(t, carry):
        tb = tt - 1 - t
        new = []
        for ci in range(nct):
            hf = a0[ci, pl.ds(t, nb, stride=stride), :] * carry[ci] + b0[ci, pl.ds(t, nb, stride=stride), :]
            o0[ci, pl.ds(t, nb, stride=stride), :] = hf
            new.append(hf)
        for ci in range(nct):
            hb = a1[ci, pl.ds(tb, nb, stride=stride), :] * carry[nct + ci] + b1[ci, pl.ds(tb, nb, stride=stride), :]
            o1[ci, pl.ds(tb, nb, stride=stride), :] = hb
            new.append(hb)
        return tuple(new)

    init = tuple(h_scr[d, :, ci * LANES:(ci + 1) * LANES] for d in range(2) for ci in range(nct))
    fin = lax.fori_loop(0, tt, step, init)
    for d in range(2):
        for ci in range(nct):
            h_scr[d, :, ci * LANES:(ci + 1) * LANES] = fin[d * nct + ci]
    for b in range(nb):
        for ci in range(nct):
            sl = slice(ci * LANES, (ci + 1) * LANES)
            of_ref[b, :, sl] = o0[ci, pl.ds(b * stride, tt), :]
            ob_ref[b, :, sl] = o1[ci, pl.ds(b * stride, tt), :]


def _rglru(xa, w_cat, b_a, b_x, sp, h0):
    bsz, length, ch = xa.shape
    tt = 128
    stride = tt + 8
    nt = length // tt
    kern = functools.partial(_lru_kernel, tt=tt, stride=stride)
    scr = pltpu.VMEM((ch // LANES, bsz * stride, LANES), F32)
    return pl.pallas_call(
        kern,
        grid=(nt,),
        in_specs=[pl.BlockSpec((bsz, tt, ch), lambda i: (0, i, 0)),
                  pl.BlockSpec((bsz, tt, ch), lambda i: (0, nt - 1 - i, 0)),
                  pl.BlockSpec((2, ch, 2 * ch), lambda i: (0, 0, 0)),
                  pl.BlockSpec((2, 1, ch), lambda i: (0, 0, 0)),
                  pl.BlockSpec((2, 1, ch), lambda i: (0, 0, 0)),
                  pl.BlockSpec((2, 1, ch), lambda i: (0, 0, 0)),
                  pl.BlockSpec((2, bsz, ch), lambda i: (0, 0, 0))],
        out_specs=[pl.BlockSpec((bsz, tt, ch), lambda i: (0, i, 0)),
                   pl.BlockSpec((bsz, tt, ch), lambda i: (0, nt - 1 - i, 0))],
        out_shape=[jax.ShapeDtypeStruct((bsz, length, ch), F32)] * 2,
        scratch_shapes=[scr, scr, scr, scr, scr, scr, pltpu.VMEM((2, bsz, ch), F32)],
        compiler_params=_params("arbitrary"),
        name="rglru",
    )(xa, xa, w_cat, b_a, b_x, sp, h0)


def _head_mask(shape, h):
    lane = lax.broadcasted_iota(jnp.int32, shape, len(shape) - 1)
    return (lane >= h * NA_HEAD_DIM) & (lane < (h + 1) * NA_HEAD_DIM)


def _softmax_pv(q, k_loc, v_loc, bias, k_ctx, v_ctx):
    dn = (((1,), (1,)), ((), ()))
    s_loc = lax.dot_general(q, k_loc, dn, preferred_element_type=F32)
    if bias is not None:
        s_loc = s_loc + bias
    s_ctx = lax.dot_general(q, k_ctx, dn, preferred_element_type=F32)
    m = jnp.maximum(jnp.max(s_loc, axis=-1, keepdims=True), jnp.max(s_ctx, axis=-1, keepdims=True))
    e_loc = jnp.exp(s_loc - m)
    e_ctx = jnp.exp(s_ctx - m)
    inv = 1.0 / (jnp.sum(e_loc, axis=-1, keepdims=True) + jnp.sum(e_ctx, axis=-1, keepdims=True))
    return (_bdot((e_loc * inv).astype(BF16), v_loc) + _bdot((e_ctx * inv).astype(BF16), v_ctx))


def _na_kernel(q_ref, k_ref, v_ref, kc_ref, vc_ref, bias_ref, o_ref, kb, vb, kcb, vcb, *, rows, rpb):
    rb = pl.program_id(2)
    kh = min(NA_KH, rows)

    @pl.when(rb == 0)
    def _():
        kb[...] = k_ref[0].astype(BF16)
        vb[...] = v_ref[0].astype(BF16)
        kcb[...] = kc_ref[0].astype(BF16)
        vcb[...] = vc_ref[0].astype(BF16)

    band = rpb + kh - 1
    u0 = jnp.clip(rb * rpb - kh // 2, 0, rows - band)
    start = pl.multiple_of(u0 * GRID_W, GRID_W)
    k_band = kb[pl.ds(start, band * GRID_W), :]
    v_band = vb[pl.ds(start, band * GRID_W), :]
    q = q_ref[0] * NA_HEAD_DIM ** -0.5
    out = None
    for h in range(LANES // NA_HEAD_DIM):
        mask = _head_mask(q.shape, h)
        qh = jnp.where(mask, q, 0.0).astype(BF16)
        oh = _softmax_pv(qh, k_band, v_band, bias_ref[0, h], kcb[...], vcb[...])
        out = oh if out is None else jnp.where(mask, oh, out)
    o_ref[0] = out


def _rpb_table_kernel(r_ref, oh_ref, ok_ref, o_ref):
    a = r_ref[...]
    a1 = a.astype(BF16)
    r1 = a - a1.astype(F32)
    a2 = r1.astype(BF16)
    a3 = (r1 - a2.astype(F32)).astype(BF16)
    oh = oh_ref[...]
    t = _bdot(a1, oh) + _bdot(a2, oh) + _bdot(a3, oh)
    o_ref[...] = jnp.where(ok_ref[...] > 0.0, t, MASK_VALUE)


def _na_bias(rpb, rows):
    cols = np.arange(GRID_W)
    col_start = np.clip(cols - NA_KW // 2, 0, GRID_W - NA_KW)
    col_ok = (cols[None, :] >= col_start[:, None]) & (cols[None, :] < col_start[:, None] + NA_KW)
    dc_idx = np.clip(cols[None, :] - cols[:, None], -(NA_KW - 1), NA_KW - 1) + (NA_KW - 1)
    n_dr, n_dc = 2 * NA_KH - 1, 2 * NA_KW - 1
    onehot = np.zeros((LANES, GRID_W * GRID_W), np.float32)
    onehot[dc_idx.reshape(-1), np.arange(GRID_W * GRID_W)] = 1.0
    ok = col_ok.reshape(1, -1).astype(np.float32)
    r2d = jnp.pad(rpb.reshape(NA_HEADS * n_dr, n_dc), ((0, LANES - NA_HEADS * n_dr), (0, LANES - n_dc)))
    tn = 1024
    table = pl.pallas_call(
        _rpb_table_kernel,
        grid=(GRID_W * GRID_W // tn,),
        in_specs=[pl.BlockSpec((LANES, LANES), lambda j: (0, 0)),
                  pl.BlockSpec((LANES, tn), lambda j: (0, j)),
                  pl.BlockSpec((1, tn), lambda j: (0, j))],
        out_specs=pl.BlockSpec((LANES, tn), lambda j: (0, j)),
        out_shape=jax.ShapeDtypeStruct((LANES, GRID_W * GRID_W), F32),
        compiler_params=_params("parallel"),
        name="rpb_table",
    )(r2d, jnp.asarray(onehot, BF16), jnp.asarray(ok))
    t = table[:NA_HEADS * n_dr].reshape(NA_HEADS, n_dr, GRID_W, GRID_W)
    t = jnp.concatenate([t, jnp.full((NA_HEADS, 1, GRID_W, GRID_W), MASK_VALUE, F32)], 1)
    nrb = rows // NA_RPB
    band = NA_RPB + NA_KH - 1
    assert nrb >= 3 and NA_RPB >= NA_KH // 2 and rows - band >= (nrb - 2) * NA_RPB - NA_KH // 2
    tile = np.full((3, NA_RPB, band), n_dr, np.int32)
    for v, rb in enumerate((0, 1, nrb - 1)):
        u0 = int(np.clip(rb * NA_RPB - NA_KH // 2, 0, rows - band))
        for i in range(NA_RPB):
            r = rb * NA_RPB + i
            r0 = int(np.clip(r - NA_KH // 2, 0, rows - NA_KH))
            for ku in range(band):
                if r0 <= u0 + ku < r0 + NA_KH:
                    tile[v, i, ku] = u0 + ku - r + NA_KH - 1
    big = jnp.take(t, jnp.asarray(tile.reshape(-1)), axis=1)
    big = big.reshape(NA_HEADS, 3, NA_RPB, band, GRID_W, GRID_W).transpose(1, 0, 2, 4, 3, 5)
    return big.reshape(3, NA_HEADS, NA_RPB * GRID_W, band * GRID_W)


def _neighbourhood_attention(proj, proj_ctx, bias):
    bsz, length, _ = proj.shape
    lc = proj_ctx.shape[1]
    rows = length // GRID_W
    kh = min(NA_KH, rows)
    nhp = D_BRANCH // LANES
    hpb = LANES // NA_HEAD_DIM
    cq, ck, cv = COL_Q // LANES, COL_K // LANES, COL_V // LANES
    rpb = NA_RPB
    nrb = rows // rpb
    tq = rpb * GRID_W
    band = rpb + kh - 1

    def bias_map(b, hp, r):
        return (jnp.where(r == 0, 0, jnp.where(r == nrb - 1, 2, 1)), hp, 0, 0)

    return pl.pallas_call(
        functools.partial(_na_kernel, rows=rows, rpb=rpb),
        grid=(bsz, nhp, nrb),
        in_specs=[pl.BlockSpec((1, tq, LANES), lambda b, hp, r: (b, r, cq + hp)),
                  pl.BlockSpec((1, length, LANES), lambda b, hp, r: (b, 0, ck + hp)),
                  pl.BlockSpec((1, length, LANES), lambda b, hp, r: (b, 0, cv + hp)),
                  pl.BlockSpec((1, lc, LANES), lambda b, hp, r: (b, 0, ck + hp)),
                  pl.BlockSpec((1, lc, LANES), lambda b, hp, r: (b, 0, cv + hp)),
                  pl.BlockSpec((1, hpb, tq, band * GRID_W), bias_map)],
        out_specs=pl.BlockSpec((1, tq, LANES), lambda b, hp, r: (b, r, hp)),
        out_shape=jax.ShapeDtypeStruct((bsz, length, D_BRANCH), F32),
        scratch_shapes=[pltpu.VMEM((length, LANES), BF16), pltpu.VMEM((length, LANES), BF16),
                        pltpu.VMEM((lc, LANES), BF16), pltpu.VMEM((lc, LANES), BF16)],
        compiler_params=_params("parallel", "parallel", "arbitrary"),
        name="neighbourhood_attention",
    )(proj, proj, proj, proj_ctx, proj_ctx, bias)


def _ctx_attn_kernel(q_ref, k_ref, v_ref, o_ref):
    q = (q_ref[0] * NA_HEAD_DIM ** -0.5).astype(BF16)
    k = k_ref[0].astype(BF16)
    v = v_ref[0].astype(BF16)
    dn = (((1,), (1,)), ((), ()))
    outs = []
    for h in range(LANES // NA_HEAD_DIM):
        sl = slice(h * NA_HEAD_DIM, (h + 1) * NA_HEAD_DIM)
        s = lax.dot_general(q[:, sl], k[:, sl], dn, preferred_element_type=F32)
        e = jnp.exp(s - jnp.max(s, axis=-1, keepdims=True))
        p = e * (1.0 / jnp.sum(e, axis=-1, keepdims=True))
        outs.append(_bdot(p.astype(BF16), v[:, sl]))
    o_ref[0] = jnp.concatenate(outs, axis=-1)


def _context_attention(proj_ctx):
    bsz, lc, _ = proj_ctx.shape
    nhp = D_BRANCH // LANES
    cq, ck, cv = COL_Q // LANES, COL_K // LANES, COL_V // LANES
    return pl.pallas_call(
        _ctx_attn_kernel,
        grid=(bsz, nhp),
        in_specs=[pl.BlockSpec((1, lc, LANES), lambda b, hp: (b, 0, cq + hp)),
                  pl.BlockSpec((1, lc, LANES), lambda b, hp: (b, 0, ck + hp)),
                  pl.BlockSpec((1, lc, LANES), lambda b, hp: (b, 0, cv + hp))],
        out_specs=pl.BlockSpec((1, lc, LANES), lambda b, hp: (b, 0, hp)),
        out_shape=jax.ShapeDtypeStruct((bsz, lc, D_BRANCH), F32),
        compiler_params=_params("parallel", "parallel"),
        name="context_attention",
    )(proj_ctx, proj_ctx, proj_ctx)


def _hy_pre_kernel(x0_ref, x1_ref, v_ref, w_ref, b_ref, x0_out, w_out):
    x0_out[0] = _dwconv(x0_ref[0], w_ref.at[0], b_ref.at[0], HY_CONV)
    x1 = _dwconv(x1_ref[0], w_ref.at[1], b_ref.at[1], HY_CONV)
    v = _dwconv(v_ref[0], w_ref.at[2], b_ref.at[2], HY_CONV)
    w_out[0] = x1 * v


def _hyena_pre(proj, conv_w, conv_b):
    bsz, length, _ = proj.shape
    nct = D_BRANCH // LANES
    c0 = COL_HY // LANES
    w3 = conv_w.reshape(HY_CONV, 3, D_BRANCH).transpose(1, 0, 2)
    b3 = conv_b.reshape(3, 1, D_BRANCH)
    spec = lambda seg: pl.BlockSpec((1, length, LANES), lambda b, c: (b, 0, c0 + seg * nct + c))
    return pl.pallas_call(
        _hy_pre_kernel,
        grid=(bsz, nct),
        in_specs=[spec(0), spec(1), spec(2),
                  pl.BlockSpec((3, HY_CONV, LANES), lambda b, c: (0, 0, c)),
                  pl.BlockSpec((3, 1, LANES), lambda b, c: (0, 0, c))],
        out_specs=[pl.BlockSpec((1, length, LANES), lambda b, c: (b, 0, c))] * 2,
        out_shape=[jax.ShapeDtypeStruct((bsz, length, D_BRANCH), F32)] * 2,
        compiler_params=_params("parallel", "parallel"),
        name="hyena_pre",
    )(proj, proj, proj, w3, b3)


def _filter_kernel(z_ref, w1_ref, b1_ref, f1_ref, w2_ref, b2_ref, f2_ref, w3_ref, dl_ref, hf_ref, hb_ref):
    z = z_ref[...]
    hid = jnp.sin(f1_ref[...] * (_dot3(z, w1_ref[...]) + b1_ref[...]))
    hid = jnp.sin(f2_ref[...] * (_dot3(hid, w2_ref[...]) + b2_ref[...]))
    filt = _dot3(hid, w3_ref[...])
    decay = jnp.exp(-z[:, 0:1] * dl_ref[...])
    hf_ref[...] = filt[:, :D_BRANCH] * decay
    hb_ref[...] = filt[:, D_BRANCH:] * decay


def _hyena_filters(length, w1, b1, freq1, w2, b2, freq2, w3):
    t = jnp.arange(length, dtype=F32) / length
    ang = 2.0 * math.pi * t[:, None] * jnp.arange(1, HY_POS_BANDS + 1, dtype=F32)
    z = jnp.concatenate([t[:, None], jnp.cos(ang), jnp.sin(ang)], -1)
    z = jnp.pad(z, ((0, 0), (0, LANES - HY_EMB)))
    w1p = jnp.pad(w1, ((0, LANES - HY_EMB), (0, 0)))
    deltas = jnp.abs(jnp.linspace(math.log(HY_DECAY_TARGET) / HY_FAST_DECAY,
                                  math.log(HY_DECAY_TARGET) / HY_SLOW_DECAY, D_BRANCH, dtype=F32))
    tl = min(length, 512)
    full = lambda a: pl.BlockSpec(a.shape, lambda i: (0,) * a.ndim)
    args = [w1p, b1.reshape(1, -1), freq1.reshape(1, -1), w2, b2.reshape(1, -1), freq2.reshape(1, -1),
            w3, deltas.reshape(1, -1)]
    return pl.pallas_call(
        _filter_kernel,
        grid=(length // tl,),
        in_specs=[pl.BlockSpec((tl, LANES), lambda i: (i, 0))] + [full(a) for a in args],
        out_specs=[pl.BlockSpec((tl, D_BRANCH), lambda i: (i, 0))] * 2,
        out_shape=[jax.ShapeDtypeStruct((length, D_BRANCH), F32)] * 2,
        compiler_params=_params("parallel"),
        name="hyena_filters",
    )(z, *args)


def _fft_tables():
    n = FFT_N1 * FFT_N2
    half = FFT_N1 // 2
    k1 = np.arange(FFT_N1, dtype=np.float64)[:, None]
    n1 = np.arange(half, dtype=np.float64)[None, :]
    t1 = np.zeros((FFT_N2, 2 * FFT_N1, 2 * half))
    for n2 in range(FFT_N2):
        th = 2.0 * np.pi * k1 * (FFT_N2 * n1 + n2) / n
        gc, gs = np.cos(th), np.sin(th)
        t1[n2] = np.block([[gc, gs], [-gs, gc]])
    t1_inv = np.transpose(t1, (0, 2, 1)) / n
    k2 = np.arange(FFT_N2, dtype=np.float64)
    th2 = 2.0 * np.pi * np.outer(k2, k2) / FFT_N2
    fc, fs = np.cos(th2), np.sin(th2)
    t2 = np.block([[fc, fs], [-fs, fc]])
    t2_inv = np.block([[fc, -fs], [fs, fc]])

    def hl(a):
        a32 = jnp.asarray(a, F32)
        hi = a32.astype(BF16)
        return hi, (a32 - hi.astype(F32)).astype(BF16)

    return hl(t1), hl(t1_inv), hl(t2), hl(t2_inv)


def _fft1_kernel(z_ref, th_ref, tl_ref, a_ref):
    cw = z_ref.shape[2] // FFT_G
    for j in range(FFT_G):
        for c in range(cw // FFT_CT):
            sl = slice(j * cw + c * FFT_CT, j * cw + (c + 1) * FFT_CT)
            d = jnp.concatenate([z_ref[0, :, sl], z_ref[1, :, sl]], axis=0)
            dh, dl = _split(d)
            a_ref[0, :, sl] = _dot3_t(th_ref[j], tl_ref[j], dh, dl)


def _fft_stage1(w, t1):
    bsz, length, ch = w.shape
    half = FFT_N1 // 2
    assert length == half * FFT_N2 and bsz % 2 == 0
    zv = w.reshape(bsz, half, FFT_N2 * ch)
    blk = FFT_G * ch
    tspec = pl.BlockSpec((FFT_G, 2 * FFT_N1, 2 * half), lambda p, g: (g, 0, 0))
    return pl.pallas_call(
        _fft1_kernel,
        grid=(bsz // 2, FFT_N2 // FFT_G),
        in_specs=[pl.BlockSpec((2, half, blk), lambda p, g: (p, 0, g)), tspec, tspec],
        out_specs=pl.BlockSpec((1, 2 * FFT_N1, blk), lambda p, g: (p, 0, g)),
        out_shape=jax.ShapeDtypeStruct((bsz // 2, 2 * FFT_N1, FFT_N2 * ch), F32),
        compiler_params=_params("parallel", "parallel"),
        name="fft_stage1",
    )(zv, *t1)


def _fft2_spec_kernel(a_ref, th_ref, tl_ref, k_ref):
    ch = a_ref.shape[4]
    for k in range(FFT_KB_SPEC):
        sign = 1.0 if k % 2 == 0 else -1.0
        for c in range(ch // FFT_CT):
            sl = slice(c * FFT_CT, (c + 1) * FFT_CT)
            acc = None
            for p in range(2):
                d = jnp.concatenate([a_ref[p, 0, k, :, sl], a_ref[p, 1, k, :, sl]], axis=0)
                dh, dl = _split(d)
                x = _dot3_t(th_ref[...], tl_ref[...], dh, dl)
                acc = x if p == 0 else acc + sign * x
            k_ref[k, :, sl] = acc


def _fft_spectrum(a_filt, t2):
    ch = a_filt.shape[2] // FFT_N2
    av = a_filt.reshape(2, 2, FFT_N1, FFT_N2, ch)
    tspec = pl.BlockSpec((2 * FFT_N2, 2 * FFT_N2), lambda i: (0, 0))
    return pl.pallas_call(
        _fft2_spec_kernel,
        grid=(FFT_N1 // FFT_KB_SPEC,),
        in_specs=[pl.BlockSpec((2, 2, FFT_KB_SPEC, FFT_N2, ch), lambda i: (0, 0, i, 0, 0)), tspec, tspec],
        out_specs=pl.BlockSpec((FFT_KB_SPEC, 2 * FFT_N2, ch), lambda i: (i, 0, 0)),
        out_shape=jax.ShapeDtypeStruct((FFT_N1, 2 * FFT_N2, ch), F32),
        compiler_params=_params("parallel"),
        name="fft_spectrum",
    )(av, *t2)


def _fft2_kernel(a_ref, k_ref, th_ref, tl_ref, ih_ref, il_ref, b_ref):
    ch = a_ref.shape[4]
    for k in range(FFT_KB):
        for c in range(ch // FFT_CT):
            sl = slice(c * FFT_CT, (c + 1) * FFT_CT)
            d = jnp.concatenate([a_ref[0, 0, k, :, sl], a_ref[0, 1, k, :, sl]], axis=0)
            dh, dl = _split(d)
            x = _dot3_t(th_ref[...], tl_ref[...], dh, dl)
            xr, xi = x[:FFT_N2], x[FFT_N2:]
            kr, ki = k_ref[k, :FFT_N2, sl], k_ref[k, FFT_N2:, sl]
            y = jnp.concatenate([xr * kr - xi * ki, xr * ki + xi * kr], axis=0)
            yh, yl = _split(y)
            out = _dot3_t(ih_ref[...], il_ref[...], yh, yl)
            b_ref[0, 0, k, :, sl] = out[:FFT_N2]
            b_ref[0, 1, k, :, sl] = out[FFT_N2:]


def _fft_stage2(a, kf, t2, t2_inv):
    npair = a.shape[0]
    ch = a.shape[2] // FFT_N2
    av = a.reshape(npair, 2, FFT_N1, FFT_N2, ch)
    tspec = pl.BlockSpec((2 * FFT_N2, 2 * FFT_N2), lambda i, p: (0, 0))
    blk = pl.BlockSpec((1, 2, FFT_KB, FFT_N2, ch), lambda i, p: (p, 0, i, 0, 0))
    out = pl.pallas_call(
        _fft2_kernel,
        grid=(FFT_N1 // FFT_KB, npair),
        in_specs=[blk, pl.BlockSpec((FFT_KB, 2 * FFT_N2, ch), lambda i, p: (i, 0, 0)),
                  tspec, tspec, tspec, tspec],
        out_specs=blk,
        out_shape=jax.ShapeDtypeStruct(av.shape, F32),
        compiler_params=_params("parallel", "parallel"),
        name="fft_stage2",
    )(av, kf, *t2, *t2_inv)
    return out.reshape(npair, 2 * FFT_N1, FFT_N2 * ch)


def _fft3_kernel(b_ref, th_ref, tl_ref, y_ref):
    half = FFT_N1 // 2
    cw = b_ref.shape[2] // FFT_G
    for j in range(FFT_G):
        for c in range(cw // FFT_CT):
            sl = slice(j * cw + c * FFT_CT, j * cw + (c + 1) * FFT_CT)
            dh, dl = _split(b_ref[0, :, sl])
            y = _dot3_t(th_ref[j], tl_ref[j], dh, dl)
            y_ref[0, :, sl] = y[:half]
            y_ref[1, :, sl] = y[half:]


def _fft_stage3(bm, t1_inv, bsz, length, ch):
    half = FFT_N1 // 2
    blk = FFT_G * ch
    tspec = pl.BlockSpec((FFT_G, 2 * half, 2 * FFT_N1), lambda p, g: (g, 0, 0))
    y = pl.pallas_call(
        _fft3_kernel,
        grid=(bsz // 2, FFT_N2 // FFT_G),
        in_specs=[pl.BlockSpec((1, 2 * FFT_N1, blk), lambda p, g: (p, 0, g)), tspec, tspec],
        out_specs=pl.BlockSpec((2, half, blk), lambda p, g: (p, 0, g)),
        out_shape=jax.ShapeDtypeStruct((bsz, half, FFT_N2 * ch), F32),
        compiler_params=_params("parallel", "parallel"),
        name="fft_stage3",
    )(bm, *t1_inv)
    return y.reshape(bsz, length, ch)


def _hyena_long_conv(w, h_fwd, h_bwd, tables):
    bsz, length, ch = w.shape
    t1, t1_inv, t2, t2_inv = tables
    zeros = jnp.zeros_like(h_fwd)
    g = jnp.concatenate([zeros[:1], jnp.flip(h_bwd[1:], 0)], 0)
    filt = jnp.stack([h_fwd, zeros, g, zeros], 0)
    kf = _fft_spectrum(_fft_stage1(filt, t1), t2)
    a = _fft_stage1(w, t1)
    bm = _fft_stage2(a, kf, t2, t2_inv)
    return _fft_stage3(bm, t1_inv, bsz, length, ch)


def _dense_dft_tables(length):
    n = 2 * length
    k = np.arange(n, dtype=np.float64)
    th = 2.0 * np.pi * np.outer(k, k) / n
    c, s = np.cos(th), np.sin(th)
    t_full = np.concatenate([c, -s], 0)
    t_fwd = t_full[:, :length]
    t_inv = np.concatenate([c[:length], -s[:length]], 1) / n
    return tuple(jnp.asarray(a, F32) for a in (t_full, t_fwd, t_inv))


def _ctx_spec_kernel(t_ref, h_ref, o_ref):
    o_ref[...] = _dot3(t_ref[...], h_ref[...])


def _ctx_conv_kernel(w_ref, k_ref, tf_ref, ti_ref, o_ref):
    n = k_ref.shape[0] // 2
    x = _dot3(tf_ref[...], w_ref[0])
    xr, xi = x[:n], x[n:]
    kr, ki = k_ref[:n], k_ref[n:]
    y = jnp.concatenate([xr * kr - xi * ki, xr * ki + xi * kr], axis=0)
    o_ref[0] = _dot3(ti_ref[...], y)


def _ctx_long_conv(w, h_fwd, h_bwd):
    bsz, length, ch = w.shape
    n = 2 * length
    t_full, t_fwd, t_inv = _dense_dft_tables(length)
    kern = jnp.concatenate([h_fwd, jnp.zeros_like(h_fwd[:1]), jnp.flip(h_bwd[1:], 0)], 0)
    kf = pl.pallas_call(
        _ctx_spec_kernel,
        grid=(1,),
        in_specs=[pl.BlockSpec((2 * n, n), lambda i: (0, 0)), pl.BlockSpec((n, ch), lambda i: (0, 0))],
        out_specs=pl.BlockSpec((2 * n, ch), lambda i: (0, 0)),
        out_shape=jax.ShapeDtypeStruct((2 * n, ch), F32),
        compiler_params=_params("arbitrary"),
        name="ctx_spectrum",
    )(t_full, kern)
    return pl.pallas_call(
        _ctx_conv_kernel,
        grid=(bsz,),
        in_specs=[pl.BlockSpec((1, length, ch), lambda b: (b, 0, 0)),
                  pl.BlockSpec((2 * n, ch), lambda b: (0, 0)),
                  pl.BlockSpec((2 * n, length), lambda b: (0, 0)),
                  pl.BlockSpec((length, 2 * n), lambda b: (0, 0))],
        out_specs=pl.BlockSpec((1, length, ch), lambda b: (b, 0, 0)),
        out_shape=jax.ShapeDtypeStruct((bsz, length, ch), F32),
        compiler_params=_params("parallel"),
        name="ctx_long_conv",
    )(w, kf, t_fwd, t_inv)


def _merge_kernel(hf_ref, hb_ref, yb_ref, cv_ref, w_ref, x0_ref, ag_ref, bg_ref, cg_ref,
                  ga_ref, gb_ref, gc_ref, h_ref, gate_ref, wb_ref, wo_ref, ds_ref, lg_ref, lb_ref,
                  o_ref, *, alpha):
    ya = hf_ref[0] + hb_ref[0]
    w = w_ref[0]
    yc = x0_ref[0] * (cv_ref[0] + ds_ref[...] * w)
    pa = _bdot((ya * _silu(ag_ref[0])).astype(BF16), wb_ref[0])
    pb = _bdot((yb_ref[0] * _silu(bg_ref[0])).astype(BF16), wb_ref[1])
    pc = _bdot((yc * _silu(cg_ref[0])).astype(BF16), wb_ref[2])
    m = _sigmoid(ga_ref[0]) * pa + _sigmoid(gb_ref[0]) * pb + _sigmoid(gc_ref[0]) * pc
    out = _bdot(m.astype(BF16), wo_ref[...])
    z = alpha * h_ref[0] + gate_ref[0] * out
    o_ref[0] = _layer_norm(z) * lg_ref[...] + lb_ref[...]


def _merge(proj, hf, hb, yb, conv, w, x0c, h, gate, wb_bf16, wo_bf16, d_skip, ln_g, ln_b, alpha):
    bsz, length, d = h.shape
    tm = 256
    half = lambda: pl.BlockSpec((1, tm, D_BRANCH), lambda b, i: (b, i, 0))
    pcol = lambda col, width: pl.BlockSpec((1, tm, width), lambda b, i: (b, i, col // width))
    row = lambda n: pl.BlockSpec((1, n), lambda b, i: (0, 0))
    return pl.pallas_call(
        functools.partial(_merge_kernel, alpha=alpha),
        grid=(bsz, length // tm),
        in_specs=[half(), half(), half(), half(), half(), half(),
                  pcol(COL_LRU_G, D_BRANCH), pcol(COL_NA_G, D_BRANCH), pcol(COL_HY_G, D_BRANCH),
                  pcol(COL_GA, D_MODEL), pcol(COL_GB, D_MODEL), pcol(COL_GC, D_MODEL),
                  pl.BlockSpec((1, tm, d), lambda b, i: (b, i, 0)),
                  pl.BlockSpec((1, 1, d), lambda b, i: (b, 0, 0)),
                  pl.BlockSpec((3, D_BRANCH, d), lambda b, i: (0, 0, 0)),
                  pl.BlockSpec((d, d), lambda b, i: (0, 0)),
                  row(D_BRANCH), row(d), row(d)],
        out_specs=pl.BlockSpec((1, tm, d), lambda b, i: (b, i, 0)),
        out_shape=jax.ShapeDtypeStruct((bsz, length, d), F32),
        compiler_params=_params("parallel", "parallel"),
        name="merge",
    )(hf, hb, yb, conv, w, x0c, proj, proj, proj, proj, proj, proj, h, gate,
      wb_bf16, wo_bf16, d_skip.reshape(1, -1), ln_g.reshape(1, -1), ln_b.reshape(1, -1))


def _block_diag(w):
    g, n, _ = w.shape
    eye = jnp.eye(g, dtype=w.dtype)
    return (eye[:, None, :, None] * w[:, :, None, :]).reshape(g * n, g * n)


def kernel(x, c, ctx, c_ctx, w_mod, b_mod, w_in, lru_conv_w, lru_conv_b, lru_w_a, lru_b_a, lru_w_x, lru_b_x,
           lru_lambda, na_rpb, hy_conv_w, hy_conv_b, hy_w1, hy_b1, hy_freq1, hy_w2, hy_b2, hy_freq2, hy_w3,
           hy_d_skip, w_branch_out, w_out, ln_g, ln_b):
    depth = w_in.shape[0]
    bsz, seq_len, d = x.shape
    ctx_len = ctx.shape[1]
    alpha = (2 * depth) ** 0.25
    tables = _fft_tables()
    pad_rows = (-(bsz + 1)) % 8
    c_rows = jnp.concatenate([c, c_ctx[None, :], jnp.zeros((pad_rows, d), F32)], 0)

    h, hc = x, ctx
    for l in range(depth):
        last = l == depth - 1
        mod = _modulation(c_rows, w_mod[l], b_mod[l])
        shift, scale, gate = (mod[:bsz, None, i * d:(i + 1) * d] for i in range(3))
        shift_x, scale_x, gate_x = (jnp.broadcast_to(mod[bsz, i * d:(i + 1) * d], (bsz, 1, d)) for i in range(3))

        w_in_l = w_in[l].astype(BF16)
        proj = _in_projection(h, scale, shift, w_in_l)
        proj_x = _in_projection(hc, scale_x, shift_x, w_in_l)

        w_cat = jnp.stack([jnp.concatenate([_block_diag(lru_w_a[l, dd]), _block_diag(lru_w_x[l, dd])], 1)
                           for dd in range(2)], 0).astype(BF16)
        b_a = lru_b_a[l][:, None, :]
        b_x = lru_b_x[l][:, None, :]
        sp = jax.nn.softplus(-lru_lambda[l])[:, None, :]
        xa = _lru_conv(proj, lru_conv_w[l], lru_conv_b[l])
        xa_x = _lru_conv(proj_x, lru_conv_w[l], lru_conv_b[l])
        hf_x, hb_x = _rglru(xa_x, w_cat, b_a, b_x, sp, jnp.zeros((2, bsz, D_BRANCH), F32))
        h0 = jnp.stack([hf_x[:, -1], hb_x[:, 0]], 0)
        hf, hb = _rglru(xa, w_cat, b_a, b_x, sp, h0)

        yb = _neighbourhood_attention(proj, proj_x, _na_bias(na_rpb[l], seq_len // GRID_W))

        filt_args = (hy_w1[l], hy_b1[l], hy_freq1[l], hy_w2[l], hy_b2[l], hy_freq2[l], hy_w3[l])
        x0c, w = _hyena_pre(proj, hy_conv_w[l], hy_conv_b[l])
        h_fwd, h_bwd = _hyena_filters(seq_len, *filt_args)
        conv = _hyena_long_conv(w, h_fwd, h_bwd, tables)

        wb = w_branch_out[l].astype(BF16)
        wo = w_out[l].astype(BF16)
        h_new = _merge(proj, hf, hb, yb, conv, w, x0c, h, gate, wb, wo, hy_d_skip[l], ln_g[l], ln_b[l], alpha)

        if not last:
            yb_x = _context_attention(proj_x)
            x0c_x, w_x = _hyena_pre(proj_x, hy_conv_w[l], hy_conv_b[l])
            hx_fwd, hx_bwd = _hyena_filters(ctx_len, *filt_args)
            conv_x = _ctx_long_conv(w_x, hx_fwd, hx_bwd)
            hc = _merge(proj_x, hf_x, hb_x, yb_x, conv_x, w_x, x0c_x, hc, gate_x, wb, wo, hy_d_skip[l],
                        ln_g[l], ln_b[l], alpha)
        h = h_new
    return h
```

```python
import functools
import math

import numpy as np
import jax
import jax.numpy as jnp
from jax import lax
from jax.experimental import pallas as pl
from jax.experimental.pallas import tpu as pltpu

F32 = jnp.float32
BF16 = jnp.bfloat16

D_MODEL = 1024
D_BRANCH = 512
N_LRU_BLOCKS = 8
LRU_CONV = 4
LRU_C = 8.0
NA_HEAD_DIM = 64
NA_HEADS = 8
NA_KH = 8
NA_KW = 16
NA_RPB = 8
NA_BAND = 16
NA_KBLK = 256
GRID_W = 64
HY_CONV = 3
HY_POS_BANDS = 16
HY_EMB = 1 + 2 * HY_POS_BANDS
HY_HID = 64
HY_DECAY_TARGET = 1e-2
HY_FAST_DECAY = 0.3
HY_SLOW_DECAY = 1.5
LN_EPS = 1e-5
D_IN = 8192

COL_LRU_X, COL_LRU_G = 0, 512
COL_Q, COL_K, COL_V, COL_NA_G = 1024, 1536, 2048, 2560
COL_HY, COL_HY_G = 3072, 4608
COL_GA, COL_GB, COL_GC = 5120, 6144, 7168

LANES = 128
VMEM_LIMIT = 56 * 1024 * 1024
MASK_VALUE = -1e30

FFT_N1 = 128
FFT_N2 = 64
FFT_G = 8
FFT_KB = 8
FFT_CT = 256


def _params(*sem):
    return pltpu.CompilerParams(dimension_semantics=sem, vmem_limit_bytes=VMEM_LIMIT)


def _split(x):
    hi = x.astype(BF16)
    lo = (x - hi.astype(F32)).astype(BF16)
    return hi, lo


def _bdot(a, b):
    return jnp.dot(a, b, preferred_element_type=F32)


def _dot3(a, b):
    ah, al = _split(a)
    bh, bl = _split(b)
    return _bdot(ah, bh) + _bdot(ah, bl) + _bdot(al, bh)


def _dot3_t(th, tl, dh, dl):
    return _bdot(th, dh) + _bdot(th, dl) + _bdot(tl, dh)


def _sigmoid(x):
    return 0.5 * jnp.tanh(0.5 * x) + 0.5


def _silu(x):
    return x * _sigmoid(x)


def _layer_norm(x):
    mu = jnp.mean(x, axis=-1, keepdims=True)
    xc = x - mu
    var = jnp.mean(xc * xc, axis=-1, keepdims=True)
    return xc * lax.rsqrt(var + LN_EPS)


def _mod_kernel(c_ref, w_ref, b_ref, o_ref):
    o_ref[...] = _dot3(_silu(c_ref[...]), w_ref[...]) + b_ref[...]


def _modulation(c_rows, w_mod, b_mod):
    rows, d = c_rows.shape
    n = w_mod.shape[1]
    tn = 512
    return pl.pallas_call(
        _mod_kernel,
        grid=(n // tn,),
        in_specs=[pl.BlockSpec((rows, d), lambda j: (0, 0)),
                  pl.BlockSpec((d, tn), lambda j: (0, j)),
                  pl.BlockSpec((1, tn), lambda j: (0, j))],
        out_specs=pl.BlockSpec((rows, tn), lambda j: (0, j)),
        out_shape=jax.ShapeDtypeStruct((rows, n), F32),
        compiler_params=_params("parallel"),
        name="modulation",
    )(c_rows, w_mod, b_mod.reshape(1, n))


def _proj_kernel(x_ref, sc_ref, sh_ref, w_ref, o_ref, u_scr):
    @pl.when(pl.program_id(2) == 0)
    def _():
        u = _layer_norm(x_ref[0]) * (1.0 + sc_ref[0]) + sh_ref[0]
        u_scr[...] = u.astype(BF16)

    o_ref[0] = _bdot(u_scr[...], w_ref[...])


def _in_projection(x, scale, shift, w_bf16):
    bsz, length, d = x.shape
    n = w_bf16.shape[1]
    tm = min(length, 2048)
    tn = 1024
    return pl.pallas_call(
        _proj_kernel,
        grid=(bsz, length // tm, n // tn),
        in_specs=[pl.BlockSpec((1, tm, d), lambda b, i, j: (b, i, 0)),
                  pl.BlockSpec((1, 1, d), lambda b, i, j: (b, 0, 0)),
                  pl.BlockSpec((1, 1, d), lambda b, i, j: (b, 0, 0)),
                  pl.BlockSpec((d, tn), lambda b, i, j: (0, j))],
        out_specs=pl.BlockSpec((1, tm, tn), lambda b, i, j: (b, i, j)),
        out_shape=jax.ShapeDtypeStruct((bsz, length, n), F32),
        scratch_shapes=[pltpu.VMEM((tm, d), BF16)],
        compiler_params=_params("parallel", "parallel", "arbitrary"),
        name="in_projection",
    )(x, scale, shift, w_bf16)


def _shift_rows(x, s):
    if s == 0:
        return x
    length = x.shape[0]
    y = pltpu.roll(x, (-s) % length, axis=0)
    rows = lax.broadcasted_iota(jnp.int32, x.shape, 0)
    keep = rows < length - s if s > 0 else rows >= -s
    return jnp.where(keep, y, 0.0)


def _dwconv(x, w_ref, b_ref, kw):
    left = kw // 2
    acc = x * w_ref[left:left + 1, :] + b_ref[...]
    for k in range(kw):
        if k != left:
            acc = acc + _shift_rows(x, k - left) * w_ref[k:k + 1, :]
    return acc


EDGE = 8


def _dwconv3(x_ref, w_ref, b_ref, kw):
    length = x_ref.shape[1]
    assert kw - 1 <= EDGE and length >= 2 * EDGE
    x = x_ref[0]
    left = kw // 2
    full = x * w_ref[left:left + 1, :] + b_ref[...]
    for k in range(kw):
        if k != left:
            full = full + pltpu.roll(x, (left - k) % length, axis=0) * w_ref[k:k + 1, :]
    top = _dwconv(x_ref[0, :2 * EDGE, :], w_ref, b_ref, kw)[:EDGE]
    bottom = _dwconv(x_ref[0, length - 2 * EDGE:, :], w_ref, b_ref, kw)[EDGE:]
    return full, top, bottom


def _store_seq(o_ref, full, top, bottom):
    length = o_ref.shape[1]
    o_ref[0] = full
    o_ref[0, :EDGE, :] = top
    o_ref[0, length - EDGE:, :] = bottom


def _lru_conv_kernel(x_ref, w_ref, b_ref, o_ref):
    _store_seq(o_ref, *_dwconv3(x_ref, w_ref, b_ref, LRU_CONV))


def _lru_conv(proj, w, b):
    bsz, length, _ = proj.shape
    nct = D_BRANCH // LANES
    c0 = COL_LRU_X // LANES
    return pl.pallas_call(
        _lru_conv_kernel,
        grid=(bsz, nct),
        in_specs=[pl.BlockSpec((1, length, LANES), lambda b, c: (b, 0, c0 + c)),
                  pl.BlockSpec((LRU_CONV, LANES), lambda b, c: (0, c)),
                  pl.BlockSpec((1, LANES), lambda b, c: (0, c))],
        out_specs=pl.BlockSpec((1, length, LANES), lambda b, c: (b, 0, c)),
        out_shape=jax.ShapeDtypeStruct((bsz, length, D_BRANCH), F32),
        compiler_params=_params("parallel", "parallel"),
        name="lru_conv",
    )(proj, w, b.reshape(1, -1))


def _lru_kernel(xf_ref, xb_ref, w_ref, ba_ref, bx_ref, sp_ref, h0_ref, of_ref, ob_ref,
                a0, a1, b0, b1, o0, o1, h_scr, *, tt, stride):
    nb = xf_ref.shape[0]
    ch = xf_ref.shape[2]
    nct = ch // LANES

    @pl.when(pl.program_id(0) == 0)
    def _():
        h_scr[...] = h0_ref[...]

    for d, x_ref, a_scr, b_scr in ((0, xf_ref, a0, b0), (1, xb_ref, a1, b1)):
        for b in range(nb):
            x = x_ref[b]
            pre = _bdot(x.astype(BF16), w_ref[d])
            ta = jnp.tanh(pre[:, :ch] + ba_ref[d])
            tx = jnp.tanh(pre[:, ch:] + bx_ref[d])
            a = jnp.exp(sp_ref[d] * (1.0 + ta))
            bb = jnp.sqrt(0.25 - 0.25 * (a * a)) * ((1.0 + tx) * x)
            for ci in range(nct):
                sl = slice(ci * LANES, (ci + 1) * LANES)
                a_scr[ci, pl.ds(b * stride, tt), :] = a[:, sl]
                b_scr[ci, pl.ds(b * stride, tt), :] = bb[:, sl]

    def step(t, carry):
        tb = tt - 1 - t
        new = []
        for ci in range(nct):
            hf = a0[ci, pl.ds(t, nb, stride=stride), :] * carry[ci] + b0[ci, pl.ds(t, nb, stride=stride), :]
            o0[ci, pl.ds(t, nb, stride=stride), :] = hf
            new.append(hf)
        for ci in range(nct):
            hb = a1[ci, pl.ds(tb, nb, stride=stride), :] * carry[nct + ci] + b1[ci, pl.ds(tb, nb, stride=stride), :]
            o1[ci, pl.ds(tb, nb, stride=stride), :] = hb
            new.append(hb)
        return tuple(new)

    init = tuple(h_scr[d, :, ci * LANES:(ci + 1) * LANES] for d in range(2) for ci in range(nct))
    fin = lax.fori_loop(0, tt, step, init)
    for d in range(2):
        for ci in range(nct):
            h_scr[d, :, ci * LANES:(ci + 1) * LANES] = fin[d * nct + ci]
    for b in range(nb):
        for ci in range(nct):
            sl = slice(ci * LANES, (ci + 1) * LANES)
            of_ref[b, :, sl] = o0[ci, pl.ds(b * stride, tt), :]
            ob_ref[b, :, sl] = o1[ci, pl.ds(b * stride, tt), :]


def _rglru(xa, w_cat, b_a, b_x, sp, h0):
    bsz, length, ch = xa.shape
    tt = 128
    stride = tt + 8
    nt = length // tt
    kern = functools.partial(_lru_kernel, tt=tt, stride=stride)
    scr = pltpu.VMEM((ch // LANES, bsz * stride, LANES), F32)
    return pl.pallas_call(
        kern,
        grid=(nt,),
        in_specs=[pl.BlockSpec((bsz, tt, ch), lambda i: (0, i, 0)),
                  pl.BlockSpec((bsz, tt, ch), lambda i: (0, nt - 1 - i, 0)),
                  pl.BlockSpec((2, ch, 2 * ch), lambda i: (0, 0, 0)),
                  pl.BlockSpec((2, 1, ch), lambda i: (0, 0, 0)),
                  pl.BlockSpec((2, 1, ch), lambda i: (0, 0, 0)),
                  pl.BlockSpec((2, 1, ch), lambda i: (0, 0, 0)),
                  pl.BlockSpec((2, bsz, ch), lambda i: (0, 0, 0))],
        out_specs=[pl.BlockSpec((bsz, tt, ch), lambda i: (0, i, 0)),
                   pl.BlockSpec((bsz, tt, ch), lambda i: (0, nt - 1 - i, 0))],
        out_shape=[jax.ShapeDtypeStruct((bsz, length, ch), F32)] * 2,
        scratch_shapes=[scr, scr, scr, scr, scr, scr, pltpu.VMEM((2, bsz, ch), F32)],
        compiler_params=_params("arbitrary"),
        name="rglru",
    )(xa, xa, w_cat, b_a, b_x, sp, h0)


def _head_mask(shape, h):
    lane = lax.broadcasted_iota(jnp.int32, shape, len(shape) - 1)
    return (lane >= h * NA_HEAD_DIM) & (lane < (h + 1) * NA_HEAD_DIM)


def _na_head(qh, k_band, v_band, bias_ref, h, kc, vc, s_scr):
    dn = (((1,), (1,)), ((), ()))
    blocks = [(k_band[o:o + NA_KBLK], v_band[o:o + NA_KBLK], o) for o in range(0, k_band.shape[0], NA_KBLK)]
    blocks += [(kc[o:o + NA_KBLK], vc[o:o + NA_KBLK], None) for o in range(0, kc.shape[0], NA_KBLK)]
    m = None
    for i, (kb, _, off) in enumerate(blocks):
        s = lax.dot_general(qh, kb, dn, preferred_element_type=F32)
        if off is not None:
            s = s + bias_ref[0, h, :, off:off + NA_KBLK]
        s_scr[:, i * NA_KBLK:(i + 1) * NA_KBLK] = s
        bm = jnp.max(s, axis=-1, keepdims=True)
        m = bm if m is None else jnp.maximum(m, bm)
    den = acc = None
    for i, (_, vb, _) in enumerate(blocks):
        e = jnp.exp(s_scr[:, i * NA_KBLK:(i + 1) * NA_KBLK] - m)
        bs = jnp.sum(e, axis=-1, keepdims=True)
        pv = _bdot(e.astype(BF16), vb)
        den = bs if den is None else den + bs
        acc = pv if acc is None else acc + pv
    return acc * (1.0 / den)


def _na_kernel(q_ref, k_ref, v_ref, kc_ref, vc_ref, bias_ref, o_ref, kb, vb, kcb, vcb, s_scr, *, rows, rpb, band):
    rb = pl.program_id(2)
    kh = min(NA_KH, rows)

    @pl.when(rb == 0)
    def _():
        kb[...] = k_ref[0].astype(BF16)
        vb[...] = v_ref[0].astype(BF16)
        kcb[...] = kc_ref[0].astype(BF16)
        vcb[...] = vc_ref[0].astype(BF16)

    u0 = jnp.clip(rb * rpb - kh // 2, 0, rows - band)
    start = pl.multiple_of(u0 * GRID_W, GRID_W)
    k_band = kb[pl.ds(start, band * GRID_W), :]
    v_band = vb[pl.ds(start, band * GRID_W), :]
    q = q_ref[0] * NA_HEAD_DIM ** -0.5
    out = None
    for h in range(LANES // NA_HEAD_DIM):
        mask = _head_mask(q.shape, h)
        qh = jnp.where(mask, q, 0.0).astype(BF16)
        oh = _na_head(qh, k_band, v_band, bias_ref, h, kcb[...], vcb[...], s_scr)
        out = oh if out is None else jnp.where(mask, oh, out)
    o_ref[0] = out


def _rpb_table_kernel(r_ref, oh_ref, ok_ref, o_ref):
    a = r_ref[...]
    a1 = a.astype(BF16)
    r1 = a - a1.astype(F32)
    a2 = r1.astype(BF16)
    a3 = (r1 - a2.astype(F32)).astype(BF16)
    oh = oh_ref[...]
    t = _bdot(a1, oh) + _bdot(a2, oh) + _bdot(a3, oh)
    o_ref[...] = jnp.where(ok_ref[...] > 0.0, t, MASK_VALUE)


def _na_bias(rpb, rows):
    cols = np.arange(GRID_W)
    col_start = np.clip(cols - NA_KW // 2, 0, GRID_W - NA_KW)
    col_ok = (cols[None, :] >= col_start[:, None]) & (cols[None, :] < col_start[:, None] + NA_KW)
    dc_idx = np.clip(cols[None, :] - cols[:, None], -(NA_KW - 1), NA_KW - 1) + (NA_KW - 1)
    n_dr, n_dc = 2 * NA_KH - 1, 2 * NA_KW - 1
    onehot = np.zeros((LANES, GRID_W * GRID_W), np.float32)
    onehot[dc_idx.reshape(-1), np.arange(GRID_W * GRID_W)] = 1.0
    ok = col_ok.reshape(1, -1).astype(np.float32)
    r2d = jnp.pad(rpb.reshape(NA_HEADS * n_dr, n_dc), ((0, LANES - NA_HEADS * n_dr), (0, LANES - n_dc)))
    tn = 1024
    table = pl.pallas_call(
        _rpb_table_kernel,
        grid=(GRID_W * GRID_W // tn,),
        in_specs=[pl.BlockSpec((LANES, LANES), lambda j: (0, 0)),
                  pl.BlockSpec((LANES, tn), lambda j: (0, j)),
                  pl.BlockSpec((1, tn), lambda j: (0, j))],
        out_specs=pl.BlockSpec((LANES, tn), lambda j: (0, j)),
        out_shape=jax.ShapeDtypeStruct((LANES, GRID_W * GRID_W), F32),
        compiler_params=_params("parallel"),
        name="rpb_table",
    )(r2d, jnp.asarray(onehot, BF16), jnp.asarray(ok))
    t = table[:NA_HEADS * n_dr].reshape(NA_HEADS, n_dr, GRID_W, GRID_W)
    t = jnp.concatenate([t, jnp.full((NA_HEADS, 1, GRID_W, GRID_W), MASK_VALUE, F32)], 1)
    nrb = rows // NA_RPB
    band = NA_BAND
    assert nrb >= 3 and NA_RPB >= NA_KH // 2 and rows - band >= (nrb - 2) * NA_RPB - NA_KH // 2
    tile = np.full((3, NA_RPB, band), n_dr, np.int32)
    for v, rb in enumerate((0, 1, nrb - 1)):
        u0 = int(np.clip(rb * NA_RPB - NA_KH // 2, 0, rows - band))
        for i in range(NA_RPB):
            r = rb * NA_RPB + i
            r0 = int(np.clip(r - NA_KH // 2, 0, rows - NA_KH))
            for ku in range(band):
                if r0 <= u0 + ku < r0 + NA_KH:
                    tile[v, i, ku] = u0 + ku - r + NA_KH - 1
    big = jnp.take(t, jnp.asarray(tile.reshape(-1)), axis=1)
    big = big.reshape(NA_HEADS, 3, NA_RPB, band, GRID_W, GRID_W).transpose(1, 0, 2, 4, 3, 5)
    return big.reshape(3, NA_HEADS, NA_RPB * GRID_W, band * GRID_W)


def _neighbourhood_attention(proj, proj_ctx, bias):
    bsz, length, _ = proj.shape
    lc = proj_ctx.shape[1]
    rows = length // GRID_W
    kh = min(NA_KH, rows)
    nhp = D_BRANCH // LANES
    hpb = LANES // NA_HEAD_DIM
    cq, ck, cv = COL_Q // LANES, COL_K // LANES, COL_V // LANES
    rpb = NA_RPB
    nrb = rows // rpb
    tq = rpb * GRID_W
    band = NA_BAND
    assert band >= rpb + kh - 1 and (band * GRID_W) % NA_KBLK == 0 and lc % NA_KBLK == 0

    def bias_map(b, hp, r):
        return (jnp.where(r == 0, 0, jnp.where(r == nrb - 1, 2, 1)), hp, 0, 0)

    return pl.pallas_call(
        functools.partial(_na_kernel, rows=rows, rpb=rpb, band=band),
        grid=(bsz, nhp, nrb),
        in_specs=[pl.BlockSpec((1, tq, LANES), lambda b, hp, r: (b, r, cq + hp)),
                  pl.BlockSpec((1, length, LANES), lambda b, hp, r: (b, 0, ck + hp)),
                  pl.BlockSpec((1, length, LANES), lambda b, hp, r: (b, 0, cv + hp)),
                  pl.BlockSpec((1, lc, LANES), lambda b, hp, r: (b, 0, ck + hp)),
                  pl.BlockSpec((1, lc, LANES), lambda b, hp, r: (b, 0, cv + hp)),
                  pl.BlockSpec((1, hpb, tq, band * GRID_W), bias_map)],
        out_specs=pl.BlockSpec((1, tq, LANES), lambda b, hp, r: (b, r, hp)),
        out_shape=jax.ShapeDtypeStruct((bsz, length, D_BRANCH), F32),
        scratch_shapes=[pltpu.VMEM((length, LANES), BF16), pltpu.VMEM((length, LANES), BF16),
                        pltpu.VMEM((lc, LANES), BF16), pltpu.VMEM((lc, LANES), BF16),
                        pltpu.VMEM((tq, band * GRID_W + lc), F32)],
        compiler_params=_params("parallel", "parallel", "arbitrary"),
        name="neighbourhood_attention",
    )(proj, proj, proj, proj_ctx, proj_ctx, bias)


def _ctx_attn_kernel(q_ref, k_ref, v_ref, o_ref):
    q = (q_ref[0] * NA_HEAD_DIM ** -0.5).astype(BF16)
    k = k_ref[0].astype(BF16)
    v = v_ref[0].astype(BF16)
    dn = (((1,), (1,)), ((), ()))
    outs = []
    for h in range(LANES // NA_HEAD_DIM):
        sl = slice(h * NA_HEAD_DIM, (h + 1) * NA_HEAD_DIM)
        s = lax.dot_general(q[:, sl], k[:, sl], dn, preferred_element_type=F32)
        e = jnp.exp(s - jnp.max(s, axis=-1, keepdims=True))
        p = e * (1.0 / jnp.sum(e, axis=-1, keepdims=True))
        outs.append(_bdot(p.astype(BF16), v[:, sl]))
    o_ref[0] = jnp.concatenate(outs, axis=-1)


def _context_attention(proj_ctx):
    bsz, lc, _ = proj_ctx.shape
    nhp = D_BRANCH // LANES
    cq, ck, cv = COL_Q // LANES, COL_K // LANES, COL_V // LANES
    return pl.pallas_call(
        _ctx_attn_kernel,
        grid=(bsz, nhp),
        in_specs=[pl.BlockSpec((1, lc, LANES), lambda b, hp: (b, 0, cq + hp)),
                  pl.BlockSpec((1, lc, LANES), lambda b, hp: (b, 0, ck + hp)),
                  pl.BlockSpec((1, lc, LANES), lambda b, hp: (b, 0, cv + hp))],
        out_specs=pl.BlockSpec((1, lc, LANES), lambda b, hp: (b, 0, hp)),
        out_shape=jax.ShapeDtypeStruct((bsz, lc, D_BRANCH), F32),
        compiler_params=_params("parallel", "parallel"),
        name="context_attention",
    )(proj_ctx, proj_ctx, proj_ctx)


def _to_grouped(x):
    n1 = x.shape[0] // FFT_N2
    ng = FFT_N2 // FFT_G
    x4 = x.reshape(n1, ng, FFT_G, x.shape[1])
    return [x4[:, g].reshape(n1 * FFT_G, x.shape[1]) for g in range(ng)]


def _from_grouped(ref, lead, rows):
    ng = FFT_N2 // FFT_G
    ntile = ref.shape[len(lead)]
    cols = []
    for t in range(ntile):
        parts = [ref[lead + (t, g, slice(i * FFT_G, (i + 1) * FFT_G), slice(None))]
                 for i in range(rows // FFT_N2) for g in range(ng)]
        cols.append(jnp.concatenate(parts, axis=0))
    return jnp.concatenate(cols, axis=1)


def _hy_pre_kernel(x0_ref, x1_ref, v_ref, w_ref, b_ref, x0_out, w_out, *, grouped):
    _store_seq(x0_out, *_dwconv3(x0_ref, w_ref.at[0], b_ref.at[0], HY_CONV))
    x1 = _dwconv3(x1_ref, w_ref.at[1], b_ref.at[1], HY_CONV)
    v = _dwconv3(v_ref, w_ref.at[2], b_ref.at[2], HY_CONV)
    full, top, bottom = (a * b for a, b in zip(x1, v))
    if grouped:
        parts = _to_grouped(full)
        for g, part in enumerate(parts):
            w_out[0, 0, g] = part
        w_out[0, 0, 0, :EDGE, :] = top
        w_out[0, 0, len(parts) - 1, parts[0].shape[0] - EDGE:, :] = bottom
    else:
        _store_seq(w_out, full, top, bottom)


def _hyena_pre(proj, conv_w, conv_b, grouped):
    bsz, length, _ = proj.shape
    nct = D_BRANCH // LANES
    c0 = COL_HY // LANES
    w3 = conv_w.reshape(HY_CONV, 3, D_BRANCH).transpose(1, 0, 2)
    b3 = conv_b.reshape(3, 1, D_BRANCH)
    spec = lambda seg: pl.BlockSpec((1, length, LANES), lambda b, c: (b, 0, c0 + seg * nct + c))
    nat_spec = pl.BlockSpec((1, length, LANES), lambda b, c: (b, 0, c))
    nat_shape = jax.ShapeDtypeStruct((bsz, length, D_BRANCH), F32)
    if grouped:
        ng = FFT_N2 // FFT_G
        w_spec = pl.BlockSpec((1, 1, ng, length // ng, LANES), lambda b, c: (b, c, 0, 0, 0))
        w_shape = jax.ShapeDtypeStruct((bsz, nct, ng, length // ng, LANES), F32)
    else:
        w_spec, w_shape = nat_spec, nat_shape
    return pl.pallas_call(
        functools.partial(_hy_pre_kernel, grouped=grouped),
        grid=(bsz, nct),
        in_specs=[spec(0), spec(1), spec(2),
                  pl.BlockSpec((3, HY_CONV, LANES), lambda b, c: (0, 0, c)),
                  pl.BlockSpec((3, 1, LANES), lambda b, c: (0, 0, c))],
        out_specs=[nat_spec, w_spec],
        out_shape=[nat_shape, w_shape],
        compiler_params=_params("parallel", "parallel"),
        name="hyena_pre",
    )(proj, proj, proj, w3, b3)


def _filter_kernel(z_ref, zg_ref, w1_ref, b1_ref, f1_ref, w2_ref, b2_ref, f2_ref, w3_ref, dl_ref,
                   *out_refs, grouped):
    def ffn(z):
        hid = jnp.sin(f1_ref[...] * (_dot3(z, w1_ref[...]) + b1_ref[...]))
        return jnp.sin(f2_ref[...] * (_dot3(hid, w2_ref[...]) + b2_ref[...]))

    z = z_ref[...]
    zg = zg_ref[...]
    tl = z.shape[0]
    h_fwd = _dot3(ffn(z), w3_ref[:, :D_BRANCH]) * jnp.exp(-z[:, 0:1] * dl_ref[...])
    row = pl.program_id(0) * tl + lax.broadcasted_iota(jnp.int32, (tl, D_BRANCH), 0)
    g = _dot3(ffn(zg), w3_ref[:, D_BRANCH:]) * jnp.exp(-zg[:, 0:1] * dl_ref[...])
    g = jnp.where(row == 0, 0.0, g)
    if grouped:
        for s, val in enumerate((h_fwd, g)):
            for t in range(D_BRANCH // LANES):
                for gi, part in enumerate(_to_grouped(val[:, t * LANES:(t + 1) * LANES])):
                    out_refs[0][s, t, gi] = part
    else:
        out_refs[0][...] = h_fwd
        out_refs[1][...] = g


def _hyena_filters(length, w1, b1, freq1, w2, b2, freq2, w3, grouped):
    t = jnp.arange(length, dtype=F32) / length
    ang = 2.0 * math.pi * t[:, None] * jnp.arange(1, HY_POS_BANDS + 1, dtype=F32)
    z = jnp.concatenate([t[:, None], jnp.cos(ang), jnp.sin(ang)], -1)
    z = jnp.pad(z, ((0, 0), (0, LANES - HY_EMB)))
    zg = jnp.concatenate([z[:1], jnp.flip(z[1:], 0)], 0)
    w1p = jnp.pad(w1, ((0, LANES - HY_EMB), (0, 0)))
    deltas = jnp.abs(jnp.linspace(math.log(HY_DECAY_TARGET) / HY_FAST_DECAY,
                                  math.log(HY_DECAY_TARGET) / HY_SLOW_DECAY, D_BRANCH, dtype=F32))
    tl = min(length, 512)
    full = lambda a: pl.BlockSpec(a.shape, lambda i: (0,) * a.ndim)
    args = [w1p, b1.reshape(1, -1), freq1.reshape(1, -1), w2, b2.reshape(1, -1), freq2.reshape(1, -1),
            w3, deltas.reshape(1, -1)]
    zspec = pl.BlockSpec((tl, LANES), lambda i: (i, 0))
    if grouped:
        nct, ng = D_BRANCH // LANES, FFT_N2 // FFT_G
        out_specs = [pl.BlockSpec((2, nct, ng, tl // ng, LANES), lambda i: (0, 0, 0, i, 0))]
        out_shape = [jax.ShapeDtypeStruct((2, nct, ng, length // ng, LANES), F32)]
    else:
        out_specs = [pl.BlockSpec((tl, D_BRANCH), lambda i: (i, 0))] * 2
        out_shape = [jax.ShapeDtypeStruct((length, D_BRANCH), F32)] * 2
    return pl.pallas_call(
        functools.partial(_filter_kernel, grouped=grouped),
        grid=(length // tl,),
        in_specs=[zspec, zspec] + [full(a) for a in args],
        out_specs=out_specs,
        out_shape=out_shape,
        compiler_params=_params("parallel"),
        name="hyena_filters",
    )(z, zg, *args)


def _fft_tables():
    n = FFT_N1 * FFT_N2
    half = FFT_N1 // 2
    k1 = np.arange(FFT_N1, dtype=np.float64)[:, None]
    n1 = np.arange(half, dtype=np.float64)[None, :]
    t1 = np.zeros((FFT_N2, 2 * FFT_N1, 2 * half))
    for n2 in range(FFT_N2):
        th = 2.0 * np.pi * k1 * (FFT_N2 * n1 + n2) / n
        gc, gs = np.cos(th), np.sin(th)
        t1[n2] = np.block([[gc, gs], [-gs, gc]])
    t1_inv = np.transpose(t1, (0, 2, 1)) / n
    sign = np.where(np.arange(2 * FFT_N1) % 2 == 0, 1.0, -1.0)[None, :, None]
    t1_filt = np.concatenate([t1[:, :, :half], sign * t1[:, :, :half]], axis=2)
    k2 = np.arange(FFT_N2, dtype=np.float64)
    th2 = 2.0 * np.pi * np.outer(k2, k2) / FFT_N2
    fc, fs = np.cos(th2), np.sin(th2)
    t2 = np.block([[fc, fs], [-fs, fc]])
    t2_inv = np.block([[fc, -fs], [fs, fc]])

    def hl(a):
        a32 = jnp.asarray(a, F32)
        hi = a32.astype(BF16)
        return hi, (a32 - hi.astype(F32)).astype(BF16)

    return hl(t1), hl(t1_inv), hl(t2), hl(t2_inv), hl(t1_filt)


TILES_PER_DOT = FFT_CT // LANES


def _lane_cat(parts):
    return parts[0] if len(parts) == 1 else jnp.concatenate(parts, axis=1)


def _fft1_kernel(z_ref, th_ref, tl_ref, a_ref):
    half = FFT_N1 // 2
    ntile = z_ref.shape[1]
    for j in range(FFT_G):
        rows = pl.ds(j, half, stride=FFT_G)
        for c in range(ntile // TILES_PER_DOT):
            tiles = range(c * TILES_PER_DOT, (c + 1) * TILES_PER_DOT)
            d = jnp.concatenate([_lane_cat([z_ref[s, t, 0, rows, :] for t in tiles]) for s in range(2)], axis=0)
            dh, dl = _split(d)
            a = _dot3_t(th_ref[j], tl_ref[j], dh, dl)
            for ri in range(2):
                for i, t in enumerate(tiles):
                    a_ref[0, 0, t, ri, pl.ds(j, FFT_N1, stride=FFT_G), :] = (
                        a[ri * FFT_N1:(ri + 1) * FFT_N1, i * LANES:(i + 1) * LANES])


def _fft_stage1(zg, t1):
    nseq, ntile, ng, rows, _ = zg.shape
    half = FFT_N1 // 2
    assert rows == half * FFT_G and ng * FFT_G == FFT_N2 and nseq % 2 == 0
    tspec = pl.BlockSpec((FFT_G, 2 * FFT_N1, 2 * half), lambda p, g: (g, 0, 0))
    return pl.pallas_call(
        _fft1_kernel,
        grid=(nseq // 2, ng),
        in_specs=[pl.BlockSpec((2, ntile, 1, rows, LANES), lambda p, g: (p, 0, g, 0, 0)), tspec, tspec],
        out_specs=pl.BlockSpec((1, 1, ntile, 2, FFT_G * FFT_N1, LANES), lambda p, g: (p, g, 0, 0, 0, 0)),
        out_shape=jax.ShapeDtypeStruct((nseq // 2, ng, ntile, 2, FFT_G * FFT_N1, LANES), F32),
        compiler_params=_params("parallel", "parallel"),
        name="fft_stage1",
    )(zg, *t1)


def _load_a(a_ref, k, tiles):
    ng = a_ref.shape[1]
    return jnp.concatenate(
        [_lane_cat([jnp.concatenate([a_ref[0, g, t, ri, k * FFT_G:(k + 1) * FFT_G, :] for g in range(ng)], axis=0)
                    for t in tiles]) for ri in range(2)], axis=0)


def _fft2_spec_kernel(a_ref, th_ref, tl_ref, k_ref):
    ntile = a_ref.shape[2]
    for k in range(FFT_KB):
        for c in range(ntile // TILES_PER_DOT):
            tiles = range(c * TILES_PER_DOT, (c + 1) * TILES_PER_DOT)
            dh, dl = _split(_load_a(a_ref, k, tiles))
            x = _dot3_t(th_ref[...], tl_ref[...], dh, dl)
            sl = slice(c * FFT_CT, (c + 1) * FFT_CT)
            k_ref[k, 0, :, sl] = x[:FFT_N2]
            k_ref[k, 1, :, sl] = x[FFT_N2:]


def _fft_spectrum(a_filt, t2):
    _, ng, ntile, _, _, _ = a_filt.shape
    ch = ntile * LANES
    tspec = pl.BlockSpec((2 * FFT_N2, 2 * FFT_N2), lambda i: (0, 0))
    return pl.pallas_call(
        _fft2_spec_kernel,
        grid=(FFT_N1 // FFT_KB,),
        in_specs=[pl.BlockSpec((1, ng, ntile, 2, FFT_KB * FFT_G, LANES), lambda i: (0, 0, 0, 0, i, 0)),
                  tspec, tspec],
        out_specs=pl.BlockSpec((FFT_KB, 2, FFT_N2, ch), lambda i: (i, 0, 0, 0)),
        out_shape=jax.ShapeDtypeStruct((FFT_N1, 2, FFT_N2, ch), F32),
        compiler_params=_params("parallel"),
        name="fft_spectrum",
    )(a_filt, *t2)


def _fft2_kernel(a_ref, k_ref, th_ref, tl_ref, ih_ref, il_ref, b_ref):
    ntile = a_ref.shape[2]
    for k in range(FFT_KB):
        for c in range(ntile // TILES_PER_DOT):
            tiles = range(c * TILES_PER_DOT, (c + 1) * TILES_PER_DOT)
            dh, dl = _split(_load_a(a_ref, k, tiles))
            x = _dot3_t(th_ref[...], tl_ref[...], dh, dl)
            xr, xi = x[:FFT_N2], x[FFT_N2:]
            sl = slice(c * FFT_CT, (c + 1) * FFT_CT)
            kr, ki = k_ref[k, 0, :, sl], k_ref[k, 1, :, sl]
            y = jnp.concatenate([xr * kr - xi * ki, xr * ki + xi * kr], axis=0)
            yh, yl = _split(y)
            out = _dot3_t(ih_ref[...], il_ref[...], yh, yl)
            for ri in range(2):
                for i, t in enumerate(tiles):
                    b_ref[0, 0, t, ri, pl.ds(k, FFT_N2, stride=FFT_KB), :] = (
                        out[ri * FFT_N2:(ri + 1) * FFT_N2, i * LANES:(i + 1) * LANES])


def _fft_stage2(a, kf, t2, t2_inv):
    npair, ng, ntile, _, _, _ = a.shape
    ch = ntile * LANES
    nkb = FFT_N1 // FFT_KB
    tspec = pl.BlockSpec((2 * FFT_N2, 2 * FFT_N2), lambda i, p: (0, 0))
    return pl.pallas_call(
        _fft2_kernel,
        grid=(nkb, npair),
        in_specs=[pl.BlockSpec((1, ng, ntile, 2, FFT_KB * FFT_G, LANES), lambda i, p: (p, 0, 0, 0, i, 0)),
                  pl.BlockSpec((FFT_KB, 2, FFT_N2, ch), lambda i, p: (i, 0, 0, 0)),
                  tspec, tspec, tspec, tspec],
        out_specs=pl.BlockSpec((1, 1, ntile, 2, FFT_KB * FFT_N2, LANES), lambda i, p: (p, i, 0, 0, 0, 0)),
        out_shape=jax.ShapeDtypeStruct((npair, nkb, ntile, 2, FFT_KB * FFT_N2, LANES), F32),
        compiler_params=_params("parallel", "parallel"),
        name="fft_stage2",
    )(a, kf, *t2, *t2_inv)


def _fft3_kernel(b_ref, th_ref, tl_ref, y_ref):
    half = FFT_N1 // 2
    nkb, ntile = b_ref.shape[1], b_ref.shape[2]
    for j in range(FFT_G):
        for c in range(ntile // TILES_PER_DOT):
            tiles = range(c * TILES_PER_DOT, (c + 1) * TILES_PER_DOT)
            d = jnp.concatenate(
                [_lane_cat([jnp.concatenate([b_ref[0, kb, t, ri, j * FFT_KB:(j + 1) * FFT_KB, :]
                                             for kb in range(nkb)], axis=0) for t in tiles])
                 for ri in range(2)], axis=0)
            dh, dl = _split(d)
            y = _dot3_t(th_ref[j], tl_ref[j], dh, dl)
            for s in range(2):
                for i, t in enumerate(tiles):
                    y_ref[s, t, 0, pl.ds(j, half, stride=FFT_G), :] = (
                        y[s * half:(s + 1) * half, i * LANES:(i + 1) * LANES])


def _fft_stage3(bm, t1_inv):
    npair, nkb, ntile, _, _, _ = bm.shape
    half = FFT_N1 // 2
    ng = FFT_N2 // FFT_G
    rows = half * FFT_G
    tspec = pl.BlockSpec((FFT_G, 2 * half, 2 * FFT_N1), lambda p, g: (g, 0, 0))
    return pl.pallas_call(
        _fft3_kernel,
        grid=(npair, ng),
        in_specs=[pl.BlockSpec((1, nkb, ntile, 2, FFT_KB * FFT_G, LANES), lambda p, g: (p, 0, 0, 0, g, 0)),
                  tspec, tspec],
        out_specs=pl.BlockSpec((2, ntile, 1, rows, LANES), lambda p, g: (p, 0, g, 0, 0)),
        out_shape=jax.ShapeDtypeStruct((2 * npair, ntile, ng, rows, LANES), F32),
        compiler_params=_params("parallel", "parallel"),
        name="fft_stage3",
    )(bm, *t1_inv)


def _hyena_long_conv(wg, filt_g, tables):
    t1, t1_inv, t2, t2_inv, t1_filt = tables
    kf = _fft_spectrum(_fft_stage1(filt_g, t1_filt), t2)
    a = _fft_stage1(wg, t1)
    bm = _fft_stage2(a, kf, t2, t2_inv)
    return _fft_stage3(bm, t1_inv)


def _dense_dft_tables(length):
    n = 2 * length
    k = np.arange(n, dtype=np.float64)
    th = 2.0 * np.pi * np.outer(k, k) / n
    c, s = np.cos(th), np.sin(th)
    t_full = np.concatenate([c, -s], 0)
    t_fwd = t_full[:, :length]
    t_inv = np.concatenate([c[:length], -s[:length]], 1) / n
    return tuple(jnp.asarray(a, F32) for a in (t_full, t_fwd, t_inv))


def _ctx_spec_kernel(t_ref, h_ref, o_ref):
    o_ref[...] = _dot3(t_ref[...], h_ref[...])


def _ctx_conv_kernel(w_ref, k_ref, tf_ref, ti_ref, o_ref):
    n = k_ref.shape[0] // 2
    x = _dot3(tf_ref[...], w_ref[0])
    xr, xi = x[:n], x[n:]
    kr, ki = k_ref[:n], k_ref[n:]
    y = jnp.concatenate([xr * kr - xi * ki, xr * ki + xi * kr], axis=0)
    o_ref[0] = _dot3(ti_ref[...], y)


def _ctx_long_conv(w, h_fwd, g):
    bsz, length, ch = w.shape
    n = 2 * length
    t_full, t_fwd, t_inv = _dense_dft_tables(length)
    kern = jnp.concatenate([h_fwd, g], 0)
    kf = pl.pallas_call(
        _ctx_spec_kernel,
        grid=(1,),
        in_specs=[pl.BlockSpec((2 * n, n), lambda i: (0, 0)), pl.BlockSpec((n, ch), lambda i: (0, 0))],
        out_specs=pl.BlockSpec((2 * n, ch), lambda i: (0, 0)),
        out_shape=jax.ShapeDtypeStruct((2 * n, ch), F32),
        compiler_params=_params("arbitrary"),
        name="ctx_spectrum",
    )(t_full, kern)
    return pl.pallas_call(
        _ctx_conv_kernel,
        grid=(bsz,),
        in_specs=[pl.BlockSpec((1, length, ch), lambda b: (b, 0, 0)),
                  pl.BlockSpec((2 * n, ch), lambda b: (0, 0)),
                  pl.BlockSpec((2 * n, length), lambda b: (0, 0)),
                  pl.BlockSpec((length, 2 * n), lambda b: (0, 0))],
        out_specs=pl.BlockSpec((1, length, ch), lambda b: (b, 0, 0)),
        out_shape=jax.ShapeDtypeStruct((bsz, length, ch), F32),
        compiler_params=_params("parallel"),
        name="ctx_long_conv",
    )(w, kf, t_fwd, t_inv)


def _merge_kernel(hf_ref, hb_ref, yb_ref, cv_ref, w_ref, x0_ref, ag_ref, bg_ref, cg_ref,
                  ga_ref, gb_ref, gc_ref, h_ref, gate_ref, wb_ref, wo_ref, ds_ref, lg_ref, lb_ref,
                  o_ref, *, alpha, grouped):
    ya = hf_ref[0] + hb_ref[0]
    if grouped:
        tm = o_ref.shape[1]
        w = _from_grouped(w_ref, (0,), tm)
        cv = _from_grouped(cv_ref, (0,), tm)
    else:
        w = w_ref[0]
        cv = cv_ref[0]
    yc = x0_ref[0] * (cv + ds_ref[...] * w)

    def gated(y, u_ref):
        u = u_ref[0]
        return ((y * u) * (1.0 + jnp.tanh(u))).astype(BF16)

    pa = _bdot(gated(ya, ag_ref), wb_ref[0])
    pb = _bdot(gated(yb_ref[0], bg_ref), wb_ref[1])
    pc = _bdot(gated(yc, cg_ref), wb_ref[2])
    m = ((1.0 + jnp.tanh(ga_ref[0])) * pa + (1.0 + jnp.tanh(gb_ref[0])) * pb
         + (1.0 + jnp.tanh(gc_ref[0])) * pc)
    out = _bdot(m.astype(BF16), wo_ref[...])
    z = alpha * h_ref[0] + gate_ref[0] * out
    o_ref[0] = _layer_norm(z) * lg_ref[...] + lb_ref[...]


def _merge(proj, hf, hb, yb, conv, w, x0c, h, gate, wb_bf16, wo_bf16, d_skip, ln_g, ln_b, alpha, grouped):
    bsz, length, d = h.shape
    tm = 256
    half = lambda: pl.BlockSpec((1, tm, D_BRANCH), lambda b, i: (b, i, 0))
    if grouped:
        ng = FFT_N2 // FFT_G
        hy = lambda: pl.BlockSpec((1, D_BRANCH // LANES, ng, tm // ng, LANES), lambda b, i: (b, 0, 0, i, 0))
    else:
        hy = half
    pcol = lambda col, width: pl.BlockSpec((1, tm, width), lambda b, i: (b, i, col // width))
    row = lambda n: pl.BlockSpec((1, n), lambda b, i: (0, 0))
    return pl.pallas_call(
        functools.partial(_merge_kernel, alpha=alpha, grouped=grouped),
        grid=(bsz, length // tm),
        in_specs=[half(), half(), half(), hy(), hy(), half(),
                  pcol(COL_LRU_G, D_BRANCH), pcol(COL_NA_G, D_BRANCH), pcol(COL_HY_G, D_BRANCH),
                  pcol(COL_GA, D_MODEL), pcol(COL_GB, D_MODEL), pcol(COL_GC, D_MODEL),
                  pl.BlockSpec((1, tm, d), lambda b, i: (b, i, 0)),
                  pl.BlockSpec((1, 1, d), lambda b, i: (b, 0, 0)),
                  pl.BlockSpec((3, D_BRANCH, d), lambda b, i: (0, 0, 0)),
                  pl.BlockSpec((d, d), lambda b, i: (0, 0)),
                  row(D_BRANCH), row(d), row(d)],
        out_specs=pl.BlockSpec((1, tm, d), lambda b, i: (b, i, 0)),
        out_shape=jax.ShapeDtypeStruct((bsz, length, d), F32),
        compiler_params=_params("parallel", "parallel"),
        name="merge",
    )(hf, hb, yb, conv, w, x0c, proj, proj, proj, proj, proj, proj, h, gate,
      wb_bf16, wo_bf16, d_skip.reshape(1, -1), ln_g.reshape(1, -1), ln_b.reshape(1, -1))


def _block_diag(w):
    g, n, _ = w.shape
    eye = jnp.eye(g, dtype=w.dtype)
    return (eye[:, None, :, None] * w[:, :, None, :]).reshape(g * n, g * n)


def kernel(x, c, ctx, c_ctx, w_mod, b_mod, w_in, lru_conv_w, lru_conv_b, lru_w_a, lru_b_a, lru_w_x, lru_b_x,
           lru_lambda, na_rpb, hy_conv_w, hy_conv_b, hy_w1, hy_b1, hy_freq1, hy_w2, hy_b2, hy_freq2, hy_w3,
           hy_d_skip, w_branch_out, w_out, ln_g, ln_b):
    depth = w_in.shape[0]
    bsz, seq_len, d = x.shape
    ctx_len = ctx.shape[1]
    alpha = (2 * depth) ** 0.25
    tables = _fft_tables()
    pad_rows = (-(bsz + 1)) % 8
    c_rows = jnp.concatenate([c, c_ctx[None, :], jnp.zeros((pad_rows, d), F32)], 0)

    col = np.arange(D_IN)
    is_gate = ((col >= COL_LRU_G) & (col < COL_Q)) | ((col >= COL_NA_G) & (col < COL_HY)) | (col >= COL_HY_G)
    gate_col_scale = jnp.asarray(np.where(is_gate, 0.5, 1.0)[None, :], F32)

    h, hc = x, ctx
    for l in range(depth):
        last = l == depth - 1
        mod = _modulation(c_rows, w_mod[l], b_mod[l])
        shift, scale, gate = (mod[:bsz, None, i * d:(i + 1) * d] for i in range(3))
        shift_x, scale_x, gate_x = (jnp.broadcast_to(mod[bsz, i * d:(i + 1) * d], (bsz, 1, d)) for i in range(3))
        gate, gate_x = 0.5 * gate, 0.5 * gate_x
        w_in_l = (w_in[l] * gate_col_scale).astype(BF16)
        proj = _in_projection(h, scale, shift, w_in_l)
        proj_x = _in_projection(hc, scale_x, shift_x, w_in_l)

        w_cat = jnp.stack([jnp.concatenate([_block_diag(lru_w_a[l, dd]), _block_diag(lru_w_x[l, dd])], 1)
                           for dd in range(2)], 0)
        w_cat = (0.5 * w_cat).astype(BF16)
        b_a = 0.5 * lru_b_a[l][:, None, :]
        b_x = 0.5 * lru_b_x[l][:, None, :]
        sp = (-0.5 * LRU_C) * jax.nn.softplus(-lru_lambda[l])[:, None, :]
        xa = _lru_conv(proj, lru_conv_w[l], lru_conv_b[l])
        xa_x = _lru_conv(proj_x, lru_conv_w[l], lru_conv_b[l])
        hf_x, hb_x = _rglru(xa_x, w_cat, b_a, b_x, sp, jnp.zeros((2, bsz, D_BRANCH), F32))
        h0 = jnp.stack([hf_x[:, -1], hb_x[:, 0]], 0)
        hf, hb = _rglru(xa, w_cat, b_a, b_x, sp, h0)

        yb = _neighbourhood_attention(proj, proj_x, _na_bias(na_rpb[l], seq_len // GRID_W))

        filt_args = (hy_w1[l], hy_b1[l], hy_freq1[l], hy_w2[l], hy_b2[l], hy_freq2[l], hy_w3[l])
        x0c, w = _hyena_pre(proj, hy_conv_w[l], hy_conv_b[l], grouped=True)
        (filt_g,) = _hyena_filters(seq_len, *filt_args, grouped=True)
        conv = _hyena_long_conv(w, filt_g, tables)

        wb = w_branch_out[l].astype(BF16)
        wo = w_out[l].astype(BF16)
        h_new = _merge(proj, hf, hb, yb, conv, w, x0c, h, gate, wb, wo, hy_d_skip[l], ln_g[l], ln_b[l], alpha,
                       grouped=True)

        if not last:
            yb_x = _context_attention(proj_x)
            x0c_x, w_x = _hyena_pre(proj_x, hy_conv_w[l], hy_conv_b[l], grouped=False)
            hx_fwd, hx_g = _hyena_filters(ctx_len, *filt_args, grouped=False)
            conv_x = _ctx_long_conv(w_x, hx_fwd, hx_g)
            hc = _merge(proj_x, hf_x, hb_x, yb_x, conv_x, w_x, x0c_x, hc, gate_x, wb, wo, hy_d_skip[l],
                        ln_g[l], ln_b[l], alpha, grouped=False)
        h = h_new
    return h
```

```python
import functools
import math

import numpy as np
import jax
import jax.numpy as jnp
from jax import lax
from jax.experimental import pallas as pl
from jax.experimental.pallas import tpu as pltpu

F32 = jnp.float32
BF16 = jnp.bfloat16

D_MODEL = 1024
D_BRANCH = 512
N_LRU_BLOCKS = 8
LRU_CONV = 4
LRU_C = 8.0
NA_HEAD_DIM = 64
NA_HEADS = 8
NA_KH = 8
NA_KW = 16
NA_RPB = 8
NA_BAND = 16
NA_KBLK = 256
GRID_W = 64
HY_CONV = 3
HY_POS_BANDS = 16
HY_EMB = 1 + 2 * HY_POS_BANDS
HY_HID = 64
HY_DECAY_TARGET = 1e-2
HY_FAST_DECAY = 0.3
HY_SLOW_DECAY = 1.5
LN_EPS = 1e-5
D_IN = 8192

COL_LRU_X, COL_LRU_G = 0, 512
COL_Q, COL_K, COL_V, COL_NA_G = 1024, 1536, 2048, 2560
COL_HY, COL_HY_G = 3072, 4608
COL_GA, COL_GB, COL_GC = 5120, 6144, 7168

LANES = 128
VMEM_LIMIT = 56 * 1024 * 1024
MASK_VALUE = -1e30

FFT_N1 = 128
FFT_N2 = 64
FFT_G = 8
FFT_KB = 8
FFT_CT = 256


def _params(*sem):
    return pltpu.CompilerParams(dimension_semantics=sem, vmem_limit_bytes=VMEM_LIMIT)


def _split(x):
    hi = x.astype(BF16)
    lo = (x - hi.astype(F32)).astype(BF16)
    return hi, lo


def _bdot(a, b):
    return jnp.dot(a, b, preferred_element_type=F32)


def _dot3(a, b):
    ah, al = _split(a)
    bh, bl = _split(b)
    return _bdot(ah, bh) + _bdot(ah, bl) + _bdot(al, bh)


def _dot3_t(th, tl, dh, dl):
    return _bdot(th, dh) + _bdot(th, dl) + _bdot(tl, dh)


def _sigmoid(x):
    return 0.5 * jnp.tanh(0.5 * x) + 0.5


def _silu(x):
    return x * _sigmoid(x)


def _layer_norm(x):
    mu = jnp.mean(x, axis=-1, keepdims=True)
    xc = x - mu
    var = jnp.mean(xc * xc, axis=-1, keepdims=True)
    return xc * lax.rsqrt(var + LN_EPS)


def _mod_kernel(c_ref, w_ref, b_ref, o_ref):
    o_ref[...] = _dot3(_silu(c_ref[...]), w_ref[...]) + b_ref[...]


def _modulation(c_rows, w_mod, b_mod):
    rows, d = c_rows.shape
    n = w_mod.shape[1]
    tn = 512
    return pl.pallas_call(
        _mod_kernel,
        grid=(n // tn,),
        in_specs=[pl.BlockSpec((rows, d), lambda j: (0, 0)),
                  pl.BlockSpec((d, tn), lambda j: (0, j)),
                  pl.BlockSpec((1, tn), lambda j: (0, j))],
        out_specs=pl.BlockSpec((rows, tn), lambda j: (0, j)),
        out_shape=jax.ShapeDtypeStruct((rows, n), F32),
        compiler_params=_params("parallel"),
        name="modulation",
    )(c_rows, w_mod, b_mod.reshape(1, n))


def _proj_kernel(x_ref, sc_ref, sh_ref, w_ref, o_ref, u_scr):
    @pl.when(pl.program_id(2) == 0)
    def _():
        u = _layer_norm(x_ref[0]) * (1.0 + sc_ref[0]) + sh_ref[0]
        u_scr[...] = u.astype(BF16)

    o_ref[0] = _bdot(u_scr[...], w_ref[...]).astype(o_ref.dtype)


def _in_projection(x, scale, shift, w_bf16):
    bsz, length, d = x.shape
    n = w_bf16.shape[1]
    tm = min(length, 2048)
    tn = 2048
    return pl.pallas_call(
        _proj_kernel,
        grid=(bsz, length // tm, n // tn),
        in_specs=[pl.BlockSpec((1, tm, d), lambda b, i, j: (b, i, 0)),
                  pl.BlockSpec((1, 1, d), lambda b, i, j: (b, 0, 0)),
                  pl.BlockSpec((1, 1, d), lambda b, i, j: (b, 0, 0)),
                  pl.BlockSpec((d, tn), lambda b, i, j: (0, j))],
        out_specs=pl.BlockSpec((1, tm, tn), lambda b, i, j: (b, i, j)),
        out_shape=jax.ShapeDtypeStruct((bsz, length, n), BF16),
        scratch_shapes=[pltpu.VMEM((tm, d), BF16)],
        compiler_params=_params("parallel", "parallel", "arbitrary"),
        name="in_projection",
    )(x, scale, shift, w_bf16)


def _shift_rows(x, s):
    if s == 0:
        return x
    length = x.shape[0]
    y = pltpu.roll(x, (-s) % length, axis=0)
    rows = lax.broadcasted_iota(jnp.int32, x.shape, 0)
    keep = rows < length - s if s > 0 else rows >= -s
    return jnp.where(keep, y, 0.0)


def _dwconv(x, w_ref, b_ref, kw):
    left = kw // 2
    acc = x * w_ref[left:left + 1, :] + b_ref[...]
    for k in range(kw):
        if k != left:
            acc = acc + _shift_rows(x, k - left) * w_ref[k:k + 1, :]
    return acc


EDGE = 8


def _dwconv3(x_ref, w_ref, b_ref, kw):
    length = x_ref.shape[1]
    assert kw - 1 <= EDGE and length >= 2 * EDGE
    x = x_ref[0].astype(F32)
    left = kw // 2
    full = x * w_ref[left:left + 1, :] + b_ref[...]
    for k in range(kw):
        if k != left:
            full = full + pltpu.roll(x, (left - k) % length, axis=0) * w_ref[k:k + 1, :]
    top = _dwconv(x_ref[0, :2 * EDGE, :].astype(F32), w_ref, b_ref, kw)[:EDGE]
    bottom = _dwconv(x_ref[0, length - 2 * EDGE:, :].astype(F32), w_ref, b_ref, kw)[EDGE:]
    return full, top, bottom


def _store_seq(o_ref, full, top, bottom):
    length = o_ref.shape[1]
    o_ref[0] = full
    o_ref[0, :EDGE, :] = top
    o_ref[0, length - EDGE:, :] = bottom


def _lru_conv_kernel(x_ref, w_ref, b_ref, o_ref):
    _store_seq(o_ref, *_dwconv3(x_ref, w_ref, b_ref, LRU_CONV))


def _lru_conv(proj, w, b):
    bsz, length, _ = proj.shape
    nct = D_BRANCH // LANES
    c0 = COL_LRU_X // LANES
    return pl.pallas_call(
        _lru_conv_kernel,
        grid=(bsz, nct),
        in_specs=[pl.BlockSpec((1, length, LANES), lambda b, c: (b, 0, c0 + c)),
                  pl.BlockSpec((LRU_CONV, LANES), lambda b, c: (0, c)),
                  pl.BlockSpec((1, LANES), lambda b, c: (0, c))],
        out_specs=pl.BlockSpec((1, length, LANES), lambda b, c: (b, 0, c)),
        out_shape=jax.ShapeDtypeStruct((bsz, length, D_BRANCH), F32),
        compiler_params=_params("parallel", "parallel"),
        name="lru_conv",
    )(proj, w, b.reshape(1, -1))


def _lru_kernel(xf_ref, xb_ref, w_ref, ba_ref, bx_ref, sp_ref, h0_ref, of_ref, ob_ref, hl_ref,
                a0, a1, b0, b1, o0, o1, h_scr, *, tt, stride):
    nb = xf_ref.shape[0]
    ch = xf_ref.shape[2]
    nct = ch // LANES

    @pl.when(pl.program_id(0) == 0)
    def _():
        h_scr[...] = h0_ref[...]

    for d, x_ref, a_scr, b_scr in ((0, xf_ref, a0, b0), (1, xb_ref, a1, b1)):
        for b in range(nb):
            x = x_ref[b]
            pre = _bdot(x.astype(BF16), w_ref[d])
            ta = jnp.tanh(pre[:, :ch] + ba_ref[d])
            tx = jnp.tanh(pre[:, ch:] + bx_ref[d])
            a = jnp.exp(sp_ref[d] * (1.0 + ta))
            bb = jnp.sqrt(0.25 - 0.25 * (a * a)) * ((1.0 + tx) * x)
            for ci in range(nct):
                sl = slice(ci * LANES, (ci + 1) * LANES)
                a_scr[ci, pl.ds(b * stride, tt), :] = a[:, sl]
                b_scr[ci, pl.ds(b * stride, tt), :] = bb[:, sl]

    def step(t, carry):
        tb = tt - 1 - t
        new = []
        for ci in range(nct):
            hf = a0[ci, pl.ds(t, nb, stride=stride), :] * carry[ci] + b0[ci, pl.ds(t, nb, stride=stride), :]
            o0[ci, pl.ds(t, nb, stride=stride), :] = hf
            new.append(hf)
        for ci in range(nct):
            hb = a1[ci, pl.ds(tb, nb, stride=stride), :] * carry[nct + ci] + b1[ci, pl.ds(tb, nb, stride=stride), :]
            o1[ci, pl.ds(tb, nb, stride=stride), :] = hb
            new.append(hb)
        return tuple(new)

    init = tuple(h_scr[d, :, ci * LANES:(ci + 1) * LANES] for d in range(2) for ci in range(nct))
    fin = lax.fori_loop(0, tt, step, init)
    for d in range(2):
        for ci in range(nct):
            h_scr[d, :, ci * LANES:(ci + 1) * LANES] = fin[d * nct + ci]
    hl_ref[...] = h_scr[...]
    for b in range(nb):
        for ci in range(nct):
            sl = slice(ci * LANES, (ci + 1) * LANES)
            of_ref[b, :, sl] = o0[ci, pl.ds(b * stride, tt), :].astype(of_ref.dtype)
            ob_ref[b, :, sl] = o1[ci, pl.ds(b * stride, tt), :].astype(ob_ref.dtype)


def _rglru(xa, w_cat, b_a, b_x, sp, h0):
    bsz, length, ch = xa.shape
    tt = 128
    stride = tt + 8
    nt = length // tt
    kern = functools.partial(_lru_kernel, tt=tt, stride=stride)
    scr = pltpu.VMEM((ch // LANES, bsz * stride, LANES), F32)
    return pl.pallas_call(
        kern,
        grid=(nt,),
        in_specs=[pl.BlockSpec((bsz, tt, ch), lambda i: (0, i, 0)),
                  pl.BlockSpec((bsz, tt, ch), lambda i: (0, nt - 1 - i, 0)),
                  pl.BlockSpec((2, ch, 2 * ch), lambda i: (0, 0, 0)),
                  pl.BlockSpec((2, 1, ch), lambda i: (0, 0, 0)),
                  pl.BlockSpec((2, 1, ch), lambda i: (0, 0, 0)),
                  pl.BlockSpec((2, 1, ch), lambda i: (0, 0, 0)),
                  pl.BlockSpec((2, bsz, ch), lambda i: (0, 0, 0))],
        out_specs=[pl.BlockSpec((bsz, tt, ch), lambda i: (0, i, 0)),
                   pl.BlockSpec((bsz, tt, ch), lambda i: (0, nt - 1 - i, 0)),
                   pl.BlockSpec((2, bsz, ch), lambda i: (0, 0, 0))],
        out_shape=[jax.ShapeDtypeStruct((bsz, length, ch), BF16)] * 2
                  + [jax.ShapeDtypeStruct((2, bsz, ch), F32)],
        scratch_shapes=[scr, scr, scr, scr, scr, scr, pltpu.VMEM((2, bsz, ch), F32)],
        compiler_params=_params("arbitrary"),
        name="rglru",
    )(xa, xa, w_cat, b_a, b_x, sp, h0)


def _head_mask(shape, h):
    lane = lax.broadcasted_iota(jnp.int32, shape, len(shape) - 1)
    return (lane >= h * NA_HEAD_DIM) & (lane < (h + 1) * NA_HEAD_DIM)


def _na_head(qh, k_band, v_band, bias_ref, h, kc, vc, s_scr):
    dn = (((1,), (1,)), ((), ()))
    blocks = [(k_band[o:o + NA_KBLK], v_band[o:o + NA_KBLK], o) for o in range(0, k_band.shape[0], NA_KBLK)]
    blocks += [(kc[o:o + NA_KBLK], vc[o:o + NA_KBLK], None) for o in range(0, kc.shape[0], NA_KBLK)]
    m = None
    for i, (kb, _, off) in enumerate(blocks):
        s = lax.dot_general(qh, kb, dn, preferred_element_type=F32)
        if off is not None:
            s = s + bias_ref[0, h, :, off:off + NA_KBLK]
        s_scr[:, i * NA_KBLK:(i + 1) * NA_KBLK] = s
        bm = jnp.max(s, axis=-1, keepdims=True)
        m = bm if m is None else jnp.maximum(m, bm)
    den = acc = None
    for i, (_, vb, _) in enumerate(blocks):
        e = jnp.exp(s_scr[:, i * NA_KBLK:(i + 1) * NA_KBLK] - m)
        bs = jnp.sum(e, axis=-1, keepdims=True)
        pv = _bdot(e.astype(BF16), vb)
        den = bs if den is None else den + bs
        acc = pv if acc is None else acc + pv
    return acc * (1.0 / den)


def _na_kernel(q_ref, k_ref, v_ref, kc_ref, vc_ref, bias_ref, o_ref, s_scr, *, rows, rpb, band):
    rb = pl.program_id(2)
    kh = min(NA_KH, rows)
    u0 = jnp.clip(rb * rpb - kh // 2, 0, rows - band)
    start = pl.multiple_of(u0 * GRID_W, GRID_W)
    k_band = k_ref[0, pl.ds(start, band * GRID_W), :]
    v_band = v_ref[0, pl.ds(start, band * GRID_W), :]
    q = q_ref[0].astype(F32) * NA_HEAD_DIM ** -0.5
    out = None
    for h in range(LANES // NA_HEAD_DIM):
        mask = _head_mask(q.shape, h)
        qh = jnp.where(mask, q, 0.0).astype(BF16)
        oh = _na_head(qh, k_band, v_band, bias_ref, h, kc_ref[0], vc_ref[0], s_scr)
        out = oh if out is None else jnp.where(mask, oh, out)
    o_ref[0] = out.astype(o_ref.dtype)


def _na_bias_kernel(r_ref, o_ref, *, tile, n_dr):
    wq = lax.broadcasted_iota(jnp.int32, (GRID_W, LANES), 0)
    lane = lax.broadcasted_iota(jnp.int32, (GRID_W, LANES), 1)
    col_start = jnp.clip(wq - NA_KW // 2, 0, GRID_W - NA_KW)
    masked = jnp.full((GRID_W, LANES), MASK_VALUE, F32)

    def toeplitz(dr, half):
        if dr == n_dr:
            return masked
        row = jnp.broadcast_to(r_ref[0, dr:dr + 1, :], (GRID_W, LANES))
        y = pltpu.roll(row, (half * GRID_W - (NA_KW - 1)) % LANES, axis=1, stride=1, stride_axis=0)
        wk = lane - half * GRID_W
        return jnp.where((wk >= col_start) & (wk < col_start + NA_KW), y, MASK_VALUE)

    nv, rpb, band = tile.shape
    for v in range(nv):
        for i in range(rpb):
            for a in range(band // 2):
                both = jnp.where(lane < GRID_W, toeplitz(int(tile[v, i, 2 * a]), 0),
                                 toeplitz(int(tile[v, i, 2 * a + 1]), 1))
                o_ref[v, 0, i * GRID_W:(i + 1) * GRID_W, a * LANES:(a + 1) * LANES] = both


def _na_bias(rpb, rows):
    n_dr, n_dc = 2 * NA_KH - 1, 2 * NA_KW - 1
    nrb = rows // NA_RPB
    band = NA_BAND
    assert 2 * GRID_W == LANES and band % 2 == 0
    assert nrb >= 3 and NA_RPB >= NA_KH // 2 and rows - band >= (nrb - 2) * NA_RPB - NA_KH // 2
    tile = np.full((3, NA_RPB, band), n_dr, np.int32)
    for v, rb in enumerate((0, 1, nrb - 1)):
        u0 = int(np.clip(rb * NA_RPB - NA_KH // 2, 0, rows - band))
        for i in range(NA_RPB):
            r = rb * NA_RPB + i
            r0 = int(np.clip(r - NA_KH // 2, 0, rows - NA_KH))
            for ku in range(band):
                if r0 <= u0 + ku < r0 + NA_KH:
                    tile[v, i, ku] = u0 + ku - r + NA_KH - 1
    r_pad = jnp.pad(rpb, ((0, 0), (0, 2 * NA_KH - n_dr), (0, LANES - n_dc)))
    tq, tk = NA_RPB * GRID_W, band * GRID_W
    return pl.pallas_call(
        functools.partial(_na_bias_kernel, tile=tile, n_dr=n_dr),
        grid=(NA_HEADS,),
        in_specs=[pl.BlockSpec((1, 2 * NA_KH, LANES), lambda h: (h, 0, 0))],
        out_specs=pl.BlockSpec((3, 1, tq, tk), lambda h: (0, h, 0, 0)),
        out_shape=jax.ShapeDtypeStruct((3, NA_HEADS, tq, tk), F32),
        compiler_params=_params("parallel"),
        name="na_bias",
    )(r_pad)


def _neighbourhood_attention(proj, proj_ctx, bias):
    bsz, length, _ = proj.shape
    lc = proj_ctx.shape[1]
    rows = length // GRID_W
    kh = min(NA_KH, rows)
    nhp = D_BRANCH // LANES
    hpb = LANES // NA_HEAD_DIM
    cq, ck, cv = COL_Q // LANES, COL_K // LANES, COL_V // LANES
    rpb = NA_RPB
    nrb = rows // rpb
    tq = rpb * GRID_W
    band = NA_BAND
    assert band >= rpb + kh - 1 and (band * GRID_W) % NA_KBLK == 0 and lc % NA_KBLK == 0

    def bias_map(b, hp, r):
        return (jnp.where(r == 0, 0, jnp.where(r == nrb - 1, 2, 1)), hp, 0, 0)

    return pl.pallas_call(
        functools.partial(_na_kernel, rows=rows, rpb=rpb, band=band),
        grid=(bsz, nhp, nrb),
        in_specs=[pl.BlockSpec((1, tq, LANES), lambda b, hp, r: (b, r, cq + hp)),
                  pl.BlockSpec((1, length, LANES), lambda b, hp, r: (b, 0, ck + hp)),
                  pl.BlockSpec((1, length, LANES), lambda b, hp, r: (b, 0, cv + hp)),
                  pl.BlockSpec((1, lc, LANES), lambda b, hp, r: (b, 0, ck + hp)),
                  pl.BlockSpec((1, lc, LANES), lambda b, hp, r: (b, 0, cv + hp)),
                  pl.BlockSpec((1, hpb, tq, band * GRID_W), bias_map)],
        out_specs=pl.BlockSpec((1, tq, LANES), lambda b, hp, r: (b, r, hp)),
        out_shape=jax.ShapeDtypeStruct((bsz, length, D_BRANCH), BF16),
        scratch_shapes=[pltpu.VMEM((tq, band * GRID_W + lc), F32)],
        compiler_params=_params("parallel", "parallel", "parallel"),
        name="neighbourhood_attention",
    )(proj, proj, proj, proj_ctx, proj_ctx, bias)


def _ctx_attn_kernel(q_ref, k_ref, v_ref, o_ref):
    q = q_ref[0].astype(F32) * NA_HEAD_DIM ** -0.5
    k = k_ref[0]
    v = v_ref[0]
    dn = (((1,), (1,)), ((), ()))
    out = None
    for h in range(LANES // NA_HEAD_DIM):
        mask = _head_mask(q.shape, h)
        qh = jnp.where(mask, q, 0.0).astype(BF16)
        s = lax.dot_general(qh, k, dn, preferred_element_type=F32)
        e = jnp.exp(s - jnp.max(s, axis=-1, keepdims=True))
        oh = _bdot(e.astype(BF16), v) * (1.0 / jnp.sum(e, axis=-1, keepdims=True))
        out = oh if out is None else jnp.where(mask, oh, out)
    o_ref[0] = out.astype(o_ref.dtype)


def _context_attention(proj_ctx):
    bsz, lc, _ = proj_ctx.shape
    nhp = D_BRANCH // LANES
    cq, ck, cv = COL_Q // LANES, COL_K // LANES, COL_V // LANES
    return pl.pallas_call(
        _ctx_attn_kernel,
        grid=(bsz, nhp),
        in_specs=[pl.BlockSpec((1, lc, LANES), lambda b, hp: (b, 0, cq + hp)),
                  pl.BlockSpec((1, lc, LANES), lambda b, hp: (b, 0, ck + hp)),
                  pl.BlockSpec((1, lc, LANES), lambda b, hp: (b, 0, cv + hp))],
        out_specs=pl.BlockSpec((1, lc, LANES), lambda b, hp: (b, 0, hp)),
        out_shape=jax.ShapeDtypeStruct((bsz, lc, D_BRANCH), BF16),
        compiler_params=_params("parallel", "parallel"),
        name="context_attention",
    )(proj_ctx, proj_ctx, proj_ctx)


def _to_grouped(x):
    n1 = x.shape[0] // FFT_N2
    ng = FFT_N2 // FFT_G
    x4 = x.reshape(n1, ng, FFT_G, x.shape[1])
    return [x4[:, g].reshape(n1 * FFT_G, x.shape[1]) for g in range(ng)]


def _from_grouped(ref, lead, rows):
    ng = FFT_N2 // FFT_G
    ntile = ref.shape[len(lead)]
    cols = []
    for t in range(ntile):
        parts = [ref[lead + (t, g, slice(i * FFT_G, (i + 1) * FFT_G), slice(None))]
                 for i in range(rows // FFT_N2) for g in range(ng)]
        cols.append(jnp.concatenate(parts, axis=0))
    return jnp.concatenate(cols, axis=1)


def _hy_pre_kernel(x0_ref, x1_ref, v_ref, w_ref, b_ref, x0_out, w_out, *, grouped):
    _store_seq(x0_out, *_dwconv3(x0_ref, w_ref.at[0], b_ref.at[0], HY_CONV))
    x1 = _dwconv3(x1_ref, w_ref.at[1], b_ref.at[1], HY_CONV)
    v = _dwconv3(v_ref, w_ref.at[2], b_ref.at[2], HY_CONV)
    full, top, bottom = (a * b for a, b in zip(x1, v))
    if grouped:
        parts = _to_grouped(full)
        for g, part in enumerate(parts):
            w_out[0, 0, g] = part
        w_out[0, 0, 0, :EDGE, :] = top
        w_out[0, 0, len(parts) - 1, parts[0].shape[0] - EDGE:, :] = bottom
    else:
        _store_seq(w_out, full, top, bottom)


def _hyena_pre(proj, conv_w, conv_b, grouped):
    bsz, length, _ = proj.shape
    nct = D_BRANCH // LANES
    c0 = COL_HY // LANES
    w3 = conv_w.reshape(HY_CONV, 3, D_BRANCH).transpose(1, 0, 2)
    b3 = conv_b.reshape(3, 1, D_BRANCH)
    spec = lambda seg: pl.BlockSpec((1, length, LANES), lambda b, c: (b, 0, c0 + seg * nct + c))
    nat_spec = pl.BlockSpec((1, length, LANES), lambda b, c: (b, 0, c))
    nat_shape = jax.ShapeDtypeStruct((bsz, length, D_BRANCH), F32)
    if grouped:
        ng = FFT_N2 // FFT_G
        w_spec = pl.BlockSpec((1, 1, ng, length // ng, LANES), lambda b, c: (b, c, 0, 0, 0))
        w_shape = jax.ShapeDtypeStruct((bsz, nct, ng, length // ng, LANES), F32)
    else:
        w_spec, w_shape = nat_spec, nat_shape
    return pl.pallas_call(
        functools.partial(_hy_pre_kernel, grouped=grouped),
        grid=(bsz, nct),
        in_specs=[spec(0), spec(1), spec(2),
                  pl.BlockSpec((3, HY_CONV, LANES), lambda b, c: (0, 0, c)),
                  pl.BlockSpec((3, 1, LANES), lambda b, c: (0, 0, c))],
        out_specs=[nat_spec, w_spec],
        out_shape=[nat_shape, w_shape],
        compiler_params=_params("parallel", "parallel"),
        name="hyena_pre",
    )(proj, proj, proj, w3, b3)


def _filter_kernel(z_ref, zg_ref, w1_ref, b1_ref, f1_ref, w2_ref, b2_ref, f2_ref, w3_ref, dl_ref,
                   *out_refs, grouped):
    def ffn(z):
        hid = jnp.sin(f1_ref[...] * (_dot3(z, w1_ref[...]) + b1_ref[...]))
        return jnp.sin(f2_ref[...] * (_dot3(hid, w2_ref[...]) + b2_ref[...]))

    z = z_ref[...]
    zg = zg_ref[...]
    tl = z.shape[0]
    h_fwd = _dot3(ffn(z), w3_ref[:, :D_BRANCH]) * jnp.exp(-z[:, 0:1] * dl_ref[...])
    row = pl.program_id(0) * tl + lax.broadcasted_iota(jnp.int32, (tl, D_BRANCH), 0)
    g = _dot3(ffn(zg), w3_ref[:, D_BRANCH:]) * jnp.exp(-zg[:, 0:1] * dl_ref[...])
    g = jnp.where(row == 0, 0.0, g)
    if grouped:
        for s, val in enumerate((h_fwd, g)):
            for t in range(D_BRANCH // LANES):
                for gi, part in enumerate(_to_grouped(val[:, t * LANES:(t + 1) * LANES])):
                    out_refs[0][s, t, gi] = part
    else:
        out_refs[0][...] = h_fwd
        out_refs[1][...] = g


def _hyena_filters(length, w1, b1, freq1, w2, b2, freq2, w3, grouped):
    t = jnp.arange(length, dtype=F32) / length
    ang = 2.0 * math.pi * t[:, None] * jnp.arange(1, HY_POS_BANDS + 1, dtype=F32)
    z = jnp.concatenate([t[:, None], jnp.cos(ang), jnp.sin(ang)], -1)
    z = jnp.pad(z, ((0, 0), (0, LANES - HY_EMB)))
    zg = jnp.concatenate([z[:1], jnp.flip(z[1:], 0)], 0)
    w1p = jnp.pad(w1, ((0, LANES - HY_EMB), (0, 0)))
    deltas = jnp.abs(jnp.linspace(math.log(HY_DECAY_TARGET) / HY_FAST_DECAY,
                                  math.log(HY_DECAY_TARGET) / HY_SLOW_DECAY, D_BRANCH, dtype=F32))
    tl = min(length, 512)
    full = lambda a: pl.BlockSpec(a.shape, lambda i: (0,) * a.ndim)
    args = [w1p, b1.reshape(1, -1), freq1.reshape(1, -1), w2, b2.reshape(1, -1), freq2.reshape(1, -1),
            w3, deltas.reshape(1, -1)]
    zspec = pl.BlockSpec((tl, LANES), lambda i: (i, 0))
    if grouped:
        nct, ng = D_BRANCH // LANES, FFT_N2 // FFT_G
        out_specs = [pl.BlockSpec((2, nct, ng, tl // ng, LANES), lambda i: (0, 0, 0, i, 0))]
        out_shape = [jax.ShapeDtypeStruct((2, nct, ng, length // ng, LANES), F32)]
    else:
        out_specs = [pl.BlockSpec((tl, D_BRANCH), lambda i: (i, 0))] * 2
        out_shape = [jax.ShapeDtypeStruct((length, D_BRANCH), F32)] * 2
    return pl.pallas_call(
        functools.partial(_filter_kernel, grouped=grouped),
        grid=(length // tl,),
        in_specs=[zspec, zspec] + [full(a) for a in args],
        out_specs=out_specs,
        out_shape=out_shape,
        compiler_params=_params("parallel"),
        name="hyena_filters",
    )(z, zg, *args)


def _fft_tables():
    n = FFT_N1 * FFT_N2
    half = FFT_N1 // 2
    k1 = np.arange(FFT_N1, dtype=np.float64)[:, None]
    n1 = np.arange(half, dtype=np.float64)[None, :]
    t1 = np.zeros((FFT_N2, 2 * FFT_N1, 2 * half))
    for n2 in range(FFT_N2):
        th = 2.0 * np.pi * k1 * (FFT_N2 * n1 + n2) / n
        gc, gs = np.cos(th), np.sin(th)
        t1[n2] = np.block([[gc, gs], [-gs, gc]])
    t1_inv = np.transpose(t1, (0, 2, 1)) / n
    sign = np.where(np.arange(2 * FFT_N1) % 2 == 0, 1.0, -1.0)[None, :, None]
    t1_filt = np.concatenate([t1[:, :, :half], sign * t1[:, :, :half]], axis=2)
    k2 = np.arange(FFT_N2, dtype=np.float64)
    th2 = 2.0 * np.pi * np.outer(k2, k2) / FFT_N2
    fc, fs = np.cos(th2), np.sin(th2)
    t2 = np.block([[fc, fs], [-fs, fc]])
    t2_inv = np.block([[fc, -fs], [fs, fc]])

    def hl(a):
        a32 = jnp.asarray(a, F32)
        hi = a32.astype(BF16)
        return hi, (a32 - hi.astype(F32)).astype(BF16)

    return hl(t1), hl(t1_inv), hl(t2), hl(t2_inv), hl(t1_filt)


TILES_PER_DOT = FFT_CT // LANES


def _lane_cat(parts):
    return parts[0] if len(parts) == 1 else jnp.concatenate(parts, axis=1)


def _fft1_kernel(z_ref, th_ref, tl_ref, a_ref):
    half = FFT_N1 // 2
    ntile = z_ref.shape[1]
    for j in range(FFT_G):
        rows = pl.ds(j, half, stride=FFT_G)
        for c in range(ntile // TILES_PER_DOT):
            tiles = range(c * TILES_PER_DOT, (c + 1) * TILES_PER_DOT)
            d = jnp.concatenate([_lane_cat([z_ref[s, t, 0, rows, :] for t in tiles]) for s in range(2)], axis=0)
            dh, dl = _split(d)
            a = _dot3_t(th_ref[j], tl_ref[j], dh, dl)
            for ri in range(2):
                for i, t in enumerate(tiles):
                    a_ref[0, 0, t, ri, pl.ds(j, FFT_N1, stride=FFT_G), :] = (
                        a[ri * FFT_N1:(ri + 1) * FFT_N1, i * LANES:(i + 1) * LANES])


def _fft_stage1(zg, t1):
    nseq, ntile, ng, rows, _ = zg.shape
    half = FFT_N1 // 2
    assert rows == half * FFT_G and ng * FFT_G == FFT_N2 and nseq % 2 == 0
    tspec = pl.BlockSpec((FFT_G, 2 * FFT_N1, 2 * half), lambda p, g: (g, 0, 0))
    return pl.pallas_call(
        _fft1_kernel,
        grid=(nseq // 2, ng),
        in_specs=[pl.BlockSpec((2, ntile, 1, rows, LANES), lambda p, g: (p, 0, g, 0, 0)), tspec, tspec],
        out_specs=pl.BlockSpec((1, 1, ntile, 2, FFT_G * FFT_N1, LANES), lambda p, g: (p, g, 0, 0, 0, 0)),
        out_shape=jax.ShapeDtypeStruct((nseq // 2, ng, ntile, 2, FFT_G * FFT_N1, LANES), F32),
        compiler_params=_params("parallel", "parallel"),
        name="fft_stage1",
    )(zg, *t1)


def _load_a(a_ref, k, tiles):
    ng = a_ref.shape[1]
    return jnp.concatenate(
        [_lane_cat([jnp.concatenate([a_ref[0, g, t, ri, k * FFT_G:(k + 1) * FFT_G, :] for g in range(ng)], axis=0)
                    for t in tiles]) for ri in range(2)], axis=0)


def _fft2_spec_kernel(a_ref, th_ref, tl_ref, k_ref):
    ntile = a_ref.shape[2]
    for k in range(FFT_KB):
        for c in range(ntile // TILES_PER_DOT):
            tiles = range(c * TILES_PER_DOT, (c + 1) * TILES_PER_DOT)
            dh, dl = _split(_load_a(a_ref, k, tiles))
            x = _dot3_t(th_ref[...], tl_ref[...], dh, dl)
            sl = slice(c * FFT_CT, (c + 1) * FFT_CT)
            k_ref[k, 0, :, sl] = x[:FFT_N2]
            k_ref[k, 1, :, sl] = x[FFT_N2:]


def _fft_spectrum(a_filt, t2):
    _, ng, ntile, _, _, _ = a_filt.shape
    ch = ntile * LANES
    tspec = pl.BlockSpec((2 * FFT_N2, 2 * FFT_N2), lambda i: (0, 0))
    return pl.pallas_call(
        _fft2_spec_kernel,
        grid=(FFT_N1 // FFT_KB,),
        in_specs=[pl.BlockSpec((1, ng, ntile, 2, FFT_KB * FFT_G, LANES), lambda i: (0, 0, 0, 0, i, 0)),
                  tspec, tspec],
        out_specs=pl.BlockSpec((FFT_KB, 2, FFT_N2, ch), lambda i: (i, 0, 0, 0)),
        out_shape=jax.ShapeDtypeStruct((FFT_N1, 2, FFT_N2, ch), F32),
        compiler_params=_params("parallel"),
        name="fft_spectrum",
    )(a_filt, *t2)


def _fft2_kernel(a_ref, k_ref, th_ref, tl_ref, ih_ref, il_ref, b_ref):
    ntile = a_ref.shape[2]
    for k in range(FFT_KB):
        for c in range(ntile // TILES_PER_DOT):
            tiles = range(c * TILES_PER_DOT, (c + 1) * TILES_PER_DOT)
            dh, dl = _split(_load_a(a_ref, k, tiles))
            x = _dot3_t(th_ref[...], tl_ref[...], dh, dl)
            xr, xi = x[:FFT_N2], x[FFT_N2:]
            sl = slice(c * FFT_CT, (c + 1) * FFT_CT)
            kr, ki = k_ref[k, 0, :, sl], k_ref[k, 1, :, sl]
            y = jnp.concatenate([xr * kr - xi * ki, xr * ki + xi * kr], axis=0)
            yh, yl = _split(y)
            out = _dot3_t(ih_ref[...], il_ref[...], yh, yl)
            for ri in range(2):
                for i, t in enumerate(tiles):
                    b_ref[0, 0, t, ri, pl.ds(k, FFT_N2, stride=FFT_KB), :] = (
                        out[ri * FFT_N2:(ri + 1) * FFT_N2, i * LANES:(i + 1) * LANES])


def _fft_stage2(a, kf, t2, t2_inv):
    npair, ng, ntile, _, _, _ = a.shape
    ch = ntile * LANES
    nkb = FFT_N1 // FFT_KB
    tspec = pl.BlockSpec((2 * FFT_N2, 2 * FFT_N2), lambda i, p: (0, 0))
    return pl.pallas_call(
        _fft2_kernel,
        grid=(nkb, npair),
        in_specs=[pl.BlockSpec((1, ng, ntile, 2, FFT_KB * FFT_G, LANES), lambda i, p: (p, 0, 0, 0, i, 0)),
                  pl.BlockSpec((FFT_KB, 2, FFT_N2, ch), lambda i, p: (i, 0, 0, 0)),
                  tspec, tspec, tspec, tspec],
        out_specs=pl.BlockSpec((1, 1, ntile, 2, FFT_KB * FFT_N2, LANES), lambda i, p: (p, i, 0, 0, 0, 0)),
        out_shape=jax.ShapeDtypeStruct((npair, nkb, ntile, 2, FFT_KB * FFT_N2, LANES), F32),
        compiler_params=_params("parallel", "parallel"),
        name="fft_stage2",
    )(a, kf, *t2, *t2_inv)


def _fft3_kernel(b_ref, th_ref, tl_ref, y_ref):
    half = FFT_N1 // 2
    nkb, ntile = b_ref.shape[1], b_ref.shape[2]
    for j in range(FFT_G):
        for c in range(ntile // TILES_PER_DOT):
            tiles = range(c * TILES_PER_DOT, (c + 1) * TILES_PER_DOT)
            d = jnp.concatenate(
                [_lane_cat([jnp.concatenate([b_ref[0, kb, t, ri, j * FFT_KB:(j + 1) * FFT_KB, :]
                                             for kb in range(nkb)], axis=0) for t in tiles])
                 for ri in range(2)], axis=0)
            dh, dl = _split(d)
            y = _dot3_t(th_ref[j], tl_ref[j], dh, dl)
            for s in range(2):
                for i, t in enumerate(tiles):
                    y_ref[s, t, 0, pl.ds(j, half, stride=FFT_G), :] = (
                        y[s * half:(s + 1) * half, i * LANES:(i + 1) * LANES])


def _fft_stage3(bm, t1_inv):
    npair, nkb, ntile, _, _, _ = bm.shape
    half = FFT_N1 // 2
    ng = FFT_N2 // FFT_G
    rows = half * FFT_G
    tspec = pl.BlockSpec((FFT_G, 2 * half, 2 * FFT_N1), lambda p, g: (g, 0, 0))
    return pl.pallas_call(
        _fft3_kernel,
        grid=(npair, ng),
        in_specs=[pl.BlockSpec((1, nkb, ntile, 2, FFT_KB * FFT_G, LANES), lambda p, g: (p, 0, 0, 0, g, 0)),
                  tspec, tspec],
        out_specs=pl.BlockSpec((2, ntile, 1, rows, LANES), lambda p, g: (p, 0, g, 0, 0)),
        out_shape=jax.ShapeDtypeStruct((2 * npair, ntile, ng, rows, LANES), F32),
        compiler_params=_params("parallel", "parallel"),
        name="fft_stage3",
    )(bm, *t1_inv)


def _hyena_long_conv(wg, filt_g, tables):
    t1, t1_inv, t2, t2_inv, t1_filt = tables
    kf = _fft_spectrum(_fft_stage1(filt_g, t1_filt), t2)
    a = _fft_stage1(wg, t1)
    bm = _fft_stage2(a, kf, t2, t2_inv)
    return _fft_stage3(bm, t1_inv)


def _dense_dft_tables(length):
    n = 2 * length
    k = np.arange(n, dtype=np.float64)
    th = 2.0 * np.pi * np.outer(k, k) / n
    c, s = np.cos(th), np.sin(th)
    t_full = np.concatenate([c, -s], 0)
    t_fwd = t_full[:, :length]
    t_inv = np.concatenate([c[:length], -s[:length]], 1) / n
    return tuple(jnp.asarray(a, F32) for a in (t_full, t_fwd, t_inv))


def _ctx_spec_kernel(t_ref, h_ref, o_ref):
    o_ref[...] = _dot3(t_ref[...], h_ref[...])


def _ctx_conv_kernel(w_ref, k_ref, tf_ref, ti_ref, o_ref):
    n = k_ref.shape[0] // 2
    x = _dot3(tf_ref[...], w_ref[0])
    xr, xi = x[:n], x[n:]
    kr, ki = k_ref[:n], k_ref[n:]
    y = jnp.concatenate([xr * kr - xi * ki, xr * ki + xi * kr], axis=0)
    o_ref[0] = _dot3(ti_ref[...], y)


def _ctx_long_conv(w, h_fwd, g):
    bsz, length, ch = w.shape
    n = 2 * length
    t_full, t_fwd, t_inv = _dense_dft_tables(length)
    kern = jnp.concatenate([h_fwd, g], 0)
    kf = pl.pallas_call(
        _ctx_spec_kernel,
        grid=(1,),
        in_specs=[pl.BlockSpec((2 * n, n), lambda i: (0, 0)), pl.BlockSpec((n, ch), lambda i: (0, 0))],
        out_specs=pl.BlockSpec((2 * n, ch), lambda i: (0, 0)),
        out_shape=jax.ShapeDtypeStruct((2 * n, ch), F32),
        compiler_params=_params("arbitrary"),
        name="ctx_spectrum",
    )(t_full, kern)
    return pl.pallas_call(
        _ctx_conv_kernel,
        grid=(bsz,),
        in_specs=[pl.BlockSpec((1, length, ch), lambda b: (b, 0, 0)),
                  pl.BlockSpec((2 * n, ch), lambda b: (0, 0)),
                  pl.BlockSpec((2 * n, length), lambda b: (0, 0)),
                  pl.BlockSpec((length, 2 * n), lambda b: (0, 0))],
        out_specs=pl.BlockSpec((1, length, ch), lambda b: (b, 0, 0)),
        out_shape=jax.ShapeDtypeStruct((bsz, length, ch), F32),
        compiler_params=_params("parallel"),
        name="ctx_long_conv",
    )(w, kf, t_fwd, t_inv)


def _merge_kernel(hf_ref, hb_ref, yb_ref, cv_ref, w_ref, x0_ref, ag_ref, bg_ref, cg_ref,
                  ga_ref, gb_ref, gc_ref, h_ref, gate_ref, wb_ref, wo_ref, ds_ref, lg_ref, lb_ref,
                  o_ref, *, alpha, grouped):
    ya = hf_ref[0].astype(F32) + hb_ref[0].astype(F32)
    if grouped:
        tm = o_ref.shape[1]
        w = _from_grouped(w_ref, (0,), tm)
        cv = _from_grouped(cv_ref, (0,), tm)
    else:
        w = w_ref[0]
        cv = cv_ref[0]
    yc = x0_ref[0] * (cv + ds_ref[...] * w)

    def gated(y, u_ref):
        u = u_ref[0].astype(F32)
        return ((y * u) * (1.0 + jnp.tanh(u))).astype(BF16)

    def sig2(u_ref):
        return 1.0 + jnp.tanh(u_ref[0].astype(F32))

    pa = _bdot(gated(ya, ag_ref), wb_ref[0])
    pb = _bdot(gated(yb_ref[0].astype(F32), bg_ref), wb_ref[1])
    pc = _bdot(gated(yc, cg_ref), wb_ref[2])
    m = sig2(ga_ref) * pa + sig2(gb_ref) * pb + sig2(gc_ref) * pc
    out = _bdot(m.astype(BF16), wo_ref[...])
    z = alpha * h_ref[0] + gate_ref[0] * out
    o_ref[0] = _layer_norm(z) * lg_ref[...] + lb_ref[...]


def _merge(proj, hf, hb, yb, conv, w, x0c, h, gate, wb_bf16, wo_bf16, d_skip, ln_g, ln_b, alpha, grouped):
    bsz, length, d = h.shape
    tm = min(length, 512)
    half = lambda: pl.BlockSpec((1, tm, D_BRANCH), lambda b, i: (b, i, 0))
    if grouped:
        ng = FFT_N2 // FFT_G
        hy = lambda: pl.BlockSpec((1, D_BRANCH // LANES, ng, tm // ng, LANES), lambda b, i: (b, 0, 0, i, 0))
    else:
        hy = half
    pcol = lambda col, width: pl.BlockSpec((1, tm, width), lambda b, i: (b, i, col // width))
    row = lambda n: pl.BlockSpec((1, n), lambda b, i: (0, 0))
    return pl.pallas_call(
        functools.partial(_merge_kernel, alpha=alpha, grouped=grouped),
        grid=(bsz, length // tm),
        in_specs=[half(), half(), half(), hy(), hy(), half(),
                  pcol(COL_LRU_G, D_BRANCH), pcol(COL_NA_G, D_BRANCH), pcol(COL_HY_G, D_BRANCH),
                  pcol(COL_GA, D_MODEL), pcol(COL_GB, D_MODEL), pcol(COL_GC, D_MODEL),
                  pl.BlockSpec((1, tm, d), lambda b, i: (b, i, 0)),
                  pl.BlockSpec((1, 1, d), lambda b, i: (b, 0, 0)),
                  pl.BlockSpec((3, D_BRANCH, d), lambda b, i: (0, 0, 0)),
                  pl.BlockSpec((d, d), lambda b, i: (0, 0)),
                  row(D_BRANCH), row(d), row(d)],
        out_specs=pl.BlockSpec((1, tm, d), lambda b, i: (b, i, 0)),
        out_shape=jax.ShapeDtypeStruct((bsz, length, d), F32),
        compiler_params=_params("parallel", "parallel"),
        name="merge",
    )(hf, hb, yb, conv, w, x0c, proj, proj, proj, proj, proj, proj, h, gate,
      wb_bf16, wo_bf16, d_skip.reshape(1, -1), ln_g.reshape(1, -1), ln_b.reshape(1, -1))


def _block_diag(w):
    g, n, _ = w.shape
    eye = jnp.eye(g, dtype=w.dtype)
    return (eye[:, None, :, None] * w[:, :, None, :]).reshape(g * n, g * n)


def kernel(x, c, ctx, c_ctx, w_mod, b_mod, w_in, lru_conv_w, lru_conv_b, lru_w_a, lru_b_a, lru_w_x, lru_b_x,
           lru_lambda, na_rpb, hy_conv_w, hy_conv_b, hy_w1, hy_b1, hy_freq1, hy_w2, hy_b2, hy_freq2, hy_w3,
           hy_d_skip, w_branch_out, w_out, ln_g, ln_b):
    depth = w_in.shape[0]
    bsz, seq_len, d = x.shape
    ctx_len = ctx.shape[1]
    alpha = (2 * depth) ** 0.25
    tables = _fft_tables()
    pad_rows = (-(bsz + 1)) % 8
    c_rows = jnp.concatenate([c, c_ctx[None, :], jnp.zeros((pad_rows, d), F32)], 0)

    col = np.arange(D_IN)
    is_gate = ((col >= COL_LRU_G) & (col < COL_Q)) | ((col >= COL_NA_G) & (col < COL_HY)) | (col >= COL_HY_G)
    gate_col_scale = jnp.asarray(np.where(is_gate, 0.5, 1.0)[None, :], F32)

    h, hc = x, ctx
    for l in range(depth):
        last = l == depth - 1
        mod = _modulation(c_rows, w_mod[l], b_mod[l])
        shift, scale, gate = (mod[:bsz, None, i * d:(i + 1) * d] for i in range(3))
        shift_x, scale_x, gate_x = (jnp.broadcast_to(mod[bsz, i * d:(i + 1) * d], (bsz, 1, d)) for i in range(3))
        gate, gate_x = 0.5 * gate, 0.5 * gate_x
        w_in_l = (w_in[l] * gate_col_scale).astype(BF16)
        proj = _in_projection(h, scale, shift, w_in_l)
        proj_x = _in_projection(hc, scale_x, shift_x, w_in_l)

        w_cat = jnp.stack([jnp.concatenate([_block_diag(lru_w_a[l, dd]), _block_diag(lru_w_x[l, dd])], 1)
                           for dd in range(2)], 0)
        w_cat = (0.5 * w_cat).astype(BF16)
        b_a = 0.5 * lru_b_a[l][:, None, :]
        b_x = 0.5 * lru_b_x[l][:, None, :]
        sp = (-0.5 * LRU_C) * jax.nn.softplus(-lru_lambda[l])[:, None, :]
        xa = _lru_conv(proj, lru_conv_w[l], lru_conv_b[l])
        xa_x = _lru_conv(proj_x, lru_conv_w[l], lru_conv_b[l])
        hf_x, hb_x, h0 = _rglru(xa_x, w_cat, b_a, b_x, sp, jnp.zeros((2, bsz, D_BRANCH), F32))
        hf, hb, _ = _rglru(xa, w_cat, b_a, b_x, sp, h0)

        yb = _neighbourhood_attention(proj, proj_x, _na_bias(na_rpb[l], seq_len // GRID_W))

        filt_args = (hy_w1[l], hy_b1[l], hy_freq1[l], hy_w2[l], hy_b2[l], hy_freq2[l], hy_w3[l])
        x0c, w = _hyena_pre(proj, hy_conv_w[l], hy_conv_b[l], grouped=True)
        (filt_g,) = _hyena_filters(seq_len, *filt_args, grouped=True)
        conv = _hyena_long_conv(w, filt_g, tables)

        wb = w_branch_out[l].astype(BF16)
        wo = w_out[l].astype(BF16)
        h_new = _merge(proj, hf, hb, yb, conv, w, x0c, h, gate, wb, wo, hy_d_skip[l], ln_g[l], ln_b[l], alpha,
                       grouped=True)

        if not last:
            yb_x = _context_attention(proj_x)
            x0c_x, w_x = _hyena_pre(proj_x, hy_conv_w[l], hy_conv_b[l], grouped=False)
            hx_fwd, hx_g = _hyena_filters(ctx_len, *filt_args, grouped=False)
            conv_x = _ctx_long_conv(w_x, hx_fwd, hx_g)
            hc = _merge(proj_x, hf_x, hb_x, yb_x, conv_x, w_x, x0c_x, hc, gate_x, wb, wo, hy_d_skip[l],
                        ln_g[l], ln_b[l], alpha, grouped=False)
        h = h_new
    return h
```

```python
import functools
import math

import numpy as np
import jax
import jax.numpy as jnp
from jax import lax
from jax.experimental import pallas as pl
from jax.experimental.pallas import tpu as pltpu

F32 = jnp.float32
BF16 = jnp.bfloat16

D_MODEL = 1024
D_BRANCH = 512
N_LRU_BLOCKS = 8
LRU_CONV = 4
LRU_C = 8.0
NA_HEAD_DIM = 64
NA_HEADS = 8
NA_KH = 8
NA_KW = 16
NA_RPB = 4
NA_BAND = 12
NA_KBLK = 256
GRID_W = 64
HY_CONV = 3
HY_POS_BANDS = 16
HY_EMB = 1 + 2 * HY_POS_BANDS
HY_HID = 64
HY_DECAY_TARGET = 1e-2
HY_FAST_DECAY = 0.3
HY_SLOW_DECAY = 1.5
LN_EPS = 1e-5
D_IN = 8192

COL_LRU_X, COL_LRU_G = 0, 512
COL_Q, COL_K, COL_V, COL_NA_G = 1024, 1536, 2048, 2560
COL_HY, COL_HY_G = 3072, 4608
COL_GA, COL_GB, COL_GC = 5120, 6144, 7168

LANES = 128
VMEM_LIMIT = 56 * 1024 * 1024
MASK_VALUE = -1e30

FFT_N1 = 128
FFT_N2 = 64
FFT_G = 8
FFT_KB = 8
FFT_CT = 256


def _params(*sem):
    return pltpu.CompilerParams(dimension_semantics=sem, vmem_limit_bytes=VMEM_LIMIT)


def _split(x):
    hi = x.astype(BF16)
    lo = (x - hi.astype(F32)).astype(BF16)
    return hi, lo


def _bdot(a, b):
    return jnp.dot(a, b, preferred_element_type=F32)


def _dot3(a, b):
    ah, al = _split(a)
    bh, bl = _split(b)
    return _bdot(ah, bh) + _bdot(ah, bl) + _bdot(al, bh)


def _dot3_t(th, tl, dh, dl):
    return _bdot(th, dh) + _bdot(th, dl) + _bdot(tl, dh)


def _sigmoid(x):
    return 0.5 * jnp.tanh(0.5 * x) + 0.5


def _silu(x):
    return x * _sigmoid(x)


def _layer_norm(x):
    mu = jnp.mean(x, axis=-1, keepdims=True)
    xc = x - mu
    var = jnp.mean(xc * xc, axis=-1, keepdims=True)
    return xc * lax.rsqrt(var + LN_EPS)


def _mod_kernel(c_ref, w_ref, b_ref, o_ref):
    o_ref[...] = _dot3(_silu(c_ref[...]), w_ref[...]) + b_ref[...]


def _modulation(c_rows, w_mod, b_mod):
    rows, d = c_rows.shape
    n = w_mod.shape[1]
    tn = 512
    return pl.pallas_call(
        _mod_kernel,
        grid=(n // tn,),
        in_specs=[pl.BlockSpec((rows, d), lambda j: (0, 0)),
                  pl.BlockSpec((d, tn), lambda j: (0, j)),
                  pl.BlockSpec((1, tn), lambda j: (0, j))],
        out_specs=pl.BlockSpec((rows, tn), lambda j: (0, j)),
        out_shape=jax.ShapeDtypeStruct((rows, n), F32),
        compiler_params=_params("parallel"),
        name="modulation",
    )(c_rows, w_mod, b_mod.reshape(1, n))


def _proj_kernel(x_ref, sc_ref, sh_ref, w_ref, o_ref, u_scr):
    @pl.when(pl.program_id(2) == 0)
    def _():
        u = _layer_norm(x_ref[0]) * (1.0 + sc_ref[0]) + sh_ref[0]
        u_scr[...] = u.astype(BF16)

    o_ref[0] = _bdot(u_scr[...], w_ref[...]).astype(o_ref.dtype)


def _in_projection(x, scale, shift, w_bf16):
    bsz, length, d = x.shape
    n = w_bf16.shape[1]
    tm = min(length, 2048)
    tn = 2048
    return pl.pallas_call(
        _proj_kernel,
        grid=(bsz, length // tm, n // tn),
        in_specs=[pl.BlockSpec((1, tm, d), lambda b, i, j: (b, i, 0)),
                  pl.BlockSpec((1, 1, d), lambda b, i, j: (b, 0, 0)),
                  pl.BlockSpec((1, 1, d), lambda b, i, j: (b, 0, 0)),
                  pl.BlockSpec((d, tn), lambda b, i, j: (0, j))],
        out_specs=pl.BlockSpec((1, tm, tn), lambda b, i, j: (b, i, j)),
        out_shape=jax.ShapeDtypeStruct((bsz, length, n), BF16),
        scratch_shapes=[pltpu.VMEM((tm, d), BF16)],
        compiler_params=_params("parallel", "parallel", "arbitrary"),
        name="in_projection",
    )(x, scale, shift, w_bf16)


PAD = 8


def _dwconv(x_ref, w_ref, b_ref, kw, xs):
    length = x_ref.shape[1]
    left = kw // 2
    assert left <= PAD and kw - 1 - left <= PAD
    zeros = jnp.zeros((PAD, xs.shape[1]), F32)
    xs[:PAD, :] = zeros
    xs[PAD + length:, :] = zeros
    xs[PAD:PAD + length, :] = x_ref[0].astype(F32)
    acc = b_ref[...] + xs[pl.ds(PAD - left, length), :] * w_ref[0:1, :]
    for k in range(1, kw):
        acc = acc + xs[pl.ds(PAD - left + k, length), :] * w_ref[k:k + 1, :]
    return acc


def _conv_scratch(length):
    return pltpu.VMEM((length + 2 * PAD, LANES), F32)


def _lru_conv_kernel(x_ref, w_ref, b_ref, o_ref, xs):
    o_ref[0] = _dwconv(x_ref, w_ref, b_ref, LRU_CONV, xs)


def _lru_conv(proj, w, b):
    bsz, length, _ = proj.shape
    nct = D_BRANCH // LANES
    c0 = COL_LRU_X // LANES
    return pl.pallas_call(
        _lru_conv_kernel,
        grid=(bsz, nct),
        in_specs=[pl.BlockSpec((1, length, LANES), lambda b, c: (b, 0, c0 + c)),
                  pl.BlockSpec((LRU_CONV, LANES), lambda b, c: (0, c)),
                  pl.BlockSpec((1, LANES), lambda b, c: (0, c))],
        out_specs=pl.BlockSpec((1, length, LANES), lambda b, c: (b, 0, c)),
        out_shape=jax.ShapeDtypeStruct((bsz, length, D_BRANCH), F32),
        scratch_shapes=[_conv_scratch(length)],
        compiler_params=_params("parallel", "parallel"),
        name="lru_conv",
    )(proj, w, b.reshape(1, -1))


def _lru_kernel(xf_ref, xb_ref, w_ref, ba_ref, bx_ref, sp_ref, h0_ref, of_ref, ob_ref, hl_ref,
                a0, a1, b0, b1, o0, o1, h_scr, *, tt, stride):
    nb = xf_ref.shape[0]
    ch = xf_ref.shape[2]
    nct = ch // LANES

    @pl.when(pl.program_id(0) == 0)
    def _():
        h_scr[...] = h0_ref[...]

    for d, x_ref, a_scr, b_scr in ((0, xf_ref, a0, b0), (1, xb_ref, a1, b1)):
        for b in range(nb):
            xh = x_ref[b]
            pre = _bdot(xh.astype(BF16), w_ref[d])
            ta = jnp.tanh(pre[:, :ch] + ba_ref[d])
            tx = jnp.tanh(pre[:, ch:] + bx_ref[d])
            a = jnp.exp(sp_ref[d] * (1.0 + ta))
            q = 1.0 - a * a
            root = jnp.where(q > 0.0, q * lax.rsqrt(q), 0.0)
            bb = root * ((1.0 + tx) * xh)
            for ci in range(nct):
                sl = slice(ci * LANES, (ci + 1) * LANES)
                a_scr[ci, pl.ds(b * stride, tt), :] = a[:, sl]
                b_scr[ci, pl.ds(b * stride, tt), :] = bb[:, sl]

    def step(t, carry):
        tb = tt - 1 - t
        new = []
        for ci in range(nct):
            hf = a0[ci, pl.ds(t, nb, stride=stride), :] * carry[ci] + b0[ci, pl.ds(t, nb, stride=stride), :]
            o0[ci, pl.ds(t, nb, stride=stride), :] = hf
            new.append(hf)
        for ci in range(nct):
            hb = a1[ci, pl.ds(tb, nb, stride=stride), :] * carry[nct + ci] + b1[ci, pl.ds(tb, nb, stride=stride), :]
            o1[ci, pl.ds(tb, nb, stride=stride), :] = hb
            new.append(hb)
        return tuple(new)

    init = tuple(h_scr[d, :, ci * LANES:(ci + 1) * LANES] for d in range(2) for ci in range(nct))
    fin = lax.fori_loop(0, tt, step, init)
    for d in range(2):
        for ci in range(nct):
            h_scr[d, :, ci * LANES:(ci + 1) * LANES] = fin[d * nct + ci]
    hl_ref[...] = h_scr[...]
    for b in range(nb):
        for ci in range(nct):
            sl = slice(ci * LANES, (ci + 1) * LANES)
            of_ref[b, :, sl] = o0[ci, pl.ds(b * stride, tt), :].astype(of_ref.dtype)
            ob_ref[b, :, sl] = o1[ci, pl.ds(b * stride, tt), :].astype(ob_ref.dtype)


def _rglru(xa, w_cat, b_a, b_x, sp, h0):
    bsz, length, ch = xa.shape
    tt = 128
    stride = tt + 8
    nt = length // tt
    kern = functools.partial(_lru_kernel, tt=tt, stride=stride)
    scr = pltpu.VMEM((ch // LANES, bsz * stride, LANES), F32)
    return pl.pallas_call(
        kern,
        grid=(nt,),
        in_specs=[pl.BlockSpec((bsz, tt, ch), lambda i: (0, i, 0)),
                  pl.BlockSpec((bsz, tt, ch), lambda i: (0, nt - 1 - i, 0)),
                  pl.BlockSpec((2, ch, 2 * ch), lambda i: (0, 0, 0)),
                  pl.BlockSpec((2, 1, ch), lambda i: (0, 0, 0)),
                  pl.BlockSpec((2, 1, ch), lambda i: (0, 0, 0)),
                  pl.BlockSpec((2, 1, ch), lambda i: (0, 0, 0)),
                  pl.BlockSpec((2, bsz, ch), lambda i: (0, 0, 0))],
        out_specs=[pl.BlockSpec((bsz, tt, ch), lambda i: (0, i, 0)),
                   pl.BlockSpec((bsz, tt, ch), lambda i: (0, nt - 1 - i, 0)),
                   pl.BlockSpec((2, bsz, ch), lambda i: (0, 0, 0))],
        out_shape=[jax.ShapeDtypeStruct((bsz, length, ch), BF16)] * 2
                  + [jax.ShapeDtypeStruct((2, bsz, ch), F32)],
        scratch_shapes=[scr, scr, scr, scr, scr, scr, pltpu.VMEM((2, bsz, ch), F32)],
        compiler_params=_params("arbitrary"),
        name="rglru",
    )(xa, xa, w_cat, b_a, b_x, sp, h0)


def _head_mask(shape, h):
    lane = lax.broadcasted_iota(jnp.int32, shape, len(shape) - 1)
    return (lane >= h * NA_HEAD_DIM) & (lane < (h + 1) * NA_HEAD_DIM)


def _na_head(qh, k_band, v_band, bias_ref, h, kc, vc, s_scr):
    dn = (((1,), (1,)), ((), ()))
    blocks = [(k_band[o:o + NA_KBLK], v_band[o:o + NA_KBLK], o) for o in range(0, k_band.shape[0], NA_KBLK)]
    blocks += [(kc[o:o + NA_KBLK], vc[o:o + NA_KBLK], None) for o in range(0, kc.shape[0], NA_KBLK)]
    m = None
    for i, (kb, _, off) in enumerate(blocks):
        s = lax.dot_general(qh, kb, dn, preferred_element_type=F32)
        if off is not None:
            s = s + bias_ref[0, h, :, off:off + NA_KBLK]
        s_scr[:, i * NA_KBLK:(i + 1) * NA_KBLK] = s
        bm = jnp.max(s, axis=-1, keepdims=True)
        m = bm if m is None else jnp.maximum(m, bm)
    den = acc = None
    for i, (_, vb, _) in enumerate(blocks):
        e = jnp.exp(s_scr[:, i * NA_KBLK:(i + 1) * NA_KBLK] - m)
        bs = jnp.sum(e, axis=-1, keepdims=True)
        pv = _bdot(e.astype(BF16), vb)
        den = bs if den is None else den + bs
        acc = pv if acc is None else acc + pv
    return acc * (1.0 / den)


def _na_kernel(q_ref, k_ref, v_ref, kc_ref, vc_ref, bias_ref, o_ref, s_scr, *, rows, rpb, band):
    rb = pl.program_id(2)
    kh = min(NA_KH, rows)
    u0 = jnp.clip(rb * rpb - kh // 2, 0, rows - band)
    start = pl.multiple_of(u0 * GRID_W, GRID_W)
    k_band = k_ref[0, pl.ds(start, band * GRID_W), :]
    v_band = v_ref[0, pl.ds(start, band * GRID_W), :]
    q = q_ref[0].astype(F32) * NA_HEAD_DIM ** -0.5
    out = None
    for h in range(LANES // NA_HEAD_DIM):
        mask = _head_mask(q.shape, h)
        qh = jnp.where(mask, q, 0.0).astype(BF16)
        oh = _na_head(qh, k_band, v_band, bias_ref, h, kc_ref[0], vc_ref[0], s_scr)
        out = oh if out is None else jnp.where(mask, oh, out)
    o_ref[0] = out.astype(o_ref.dtype)


def _na_bias_kernel(r_ref, o_ref, *, tile, n_dr):
    wq = lax.broadcasted_iota(jnp.int32, (GRID_W, LANES), 0)
    lane = lax.broadcasted_iota(jnp.int32, (GRID_W, LANES), 1)
    col_start = jnp.clip(wq - NA_KW // 2, 0, GRID_W - NA_KW)
    masked = jnp.full((GRID_W, LANES), MASK_VALUE, F32)

    def toeplitz(dr, half):
        if dr == n_dr:
            return masked
        row = jnp.broadcast_to(r_ref[0, dr:dr + 1, :], (GRID_W, LANES))
        y = pltpu.roll(row, (half * GRID_W - (NA_KW - 1)) % LANES, axis=1, stride=1, stride_axis=0)
        wk = lane - half * GRID_W
        return jnp.where((wk >= col_start) & (wk < col_start + NA_KW), y, MASK_VALUE)

    nv, rpb, band = tile.shape
    for v in range(nv):
        for i in range(rpb):
            for a in range(band // 2):
                both = jnp.where(lane < GRID_W, toeplitz(int(tile[v, i, 2 * a]), 0),
                                 toeplitz(int(tile[v, i, 2 * a + 1]), 1))
                o_ref[v, 0, i * GRID_W:(i + 1) * GRID_W, a * LANES:(a + 1) * LANES] = both


def _na_bias(rpb, rows):
    n_dr, n_dc = 2 * NA_KH - 1, 2 * NA_KW - 1
    nrb = rows // NA_RPB
    band = NA_BAND
    assert 2 * GRID_W == LANES and band % 2 == 0
    assert nrb >= 3 and NA_RPB >= NA_KH // 2 and rows - band >= (nrb - 2) * NA_RPB - NA_KH // 2
    tile = np.full((3, NA_RPB, band), n_dr, np.int32)
    for v, rb in enumerate((0, 1, nrb - 1)):
        u0 = int(np.clip(rb * NA_RPB - NA_KH // 2, 0, rows - band))
        for i in range(NA_RPB):
            r = rb * NA_RPB + i
            r0 = int(np.clip(r - NA_KH // 2, 0, rows - NA_KH))
            for ku in range(band):
                if r0 <= u0 + ku < r0 + NA_KH:
                    tile[v, i, ku] = u0 + ku - r + NA_KH - 1
    r_pad = jnp.pad(rpb, ((0, 0), (0, 2 * NA_KH - n_dr), (0, LANES - n_dc)))
    tq, tk = NA_RPB * GRID_W, band * GRID_W
    return pl.pallas_call(
        functools.partial(_na_bias_kernel, tile=tile, n_dr=n_dr),
        grid=(NA_HEADS,),
        in_specs=[pl.BlockSpec((1, 2 * NA_KH, LANES), lambda h: (h, 0, 0))],
        out_specs=pl.BlockSpec((3, 1, tq, tk), lambda h: (0, h, 0, 0)),
        out_shape=jax.ShapeDtypeStruct((3, NA_HEADS, tq, tk), F32),
        compiler_params=_params("parallel"),
        name="na_bias",
    )(r_pad)


def _neighbourhood_attention(proj, proj_ctx, bias):
    bsz, length, _ = proj.shape
    lc = proj_ctx.shape[1]
    rows = length // GRID_W
    kh = min(NA_KH, rows)
    nhp = D_BRANCH // LANES
    hpb = LANES // NA_HEAD_DIM
    cq, ck, cv = COL_Q // LANES, COL_K // LANES, COL_V // LANES
    rpb = NA_RPB
    nrb = rows // rpb
    tq = rpb * GRID_W
    band = NA_BAND
    assert band >= rpb + kh - 1 and (band * GRID_W) % NA_KBLK == 0 and lc % NA_KBLK == 0

    def bias_map(b, hp, r):
        return (jnp.where(r == 0, 0, jnp.where(r == nrb - 1, 2, 1)), hp, 0, 0)

    return pl.pallas_call(
        functools.partial(_na_kernel, rows=rows, rpb=rpb, band=band),
        grid=(bsz, nhp, nrb),
        in_specs=[pl.BlockSpec((1, tq, LANES), lambda b, hp, r: (b, r, cq + hp)),
                  pl.BlockSpec((1, length, LANES), lambda b, hp, r: (b, 0, ck + hp)),
                  pl.BlockSpec((1, length, LANES), lambda b, hp, r: (b, 0, cv + hp)),
                  pl.BlockSpec((1, lc, LANES), lambda b, hp, r: (b, 0, ck + hp)),
                  pl.BlockSpec((1, lc, LANES), lambda b, hp, r: (b, 0, cv + hp)),
                  pl.BlockSpec((1, hpb, tq, band * GRID_W), bias_map)],
        out_specs=pl.BlockSpec((1, tq, LANES), lambda b, hp, r: (b, r, hp)),
        out_shape=jax.ShapeDtypeStruct((bsz, length, D_BRANCH), BF16),
        scratch_shapes=[pltpu.VMEM((tq, band * GRID_W + lc), F32)],
        compiler_params=_params("parallel", "parallel", "parallel"),
        name="neighbourhood_attention",
    )(proj, proj, proj, proj_ctx, proj_ctx, bias)


def _ctx_attn_kernel(q_ref, k_ref, v_ref, o_ref):
    q = q_ref[0].astype(F32) * NA_HEAD_DIM ** -0.5
    k = k_ref[0]
    v = v_ref[0]
    dn = (((1,), (1,)), ((), ()))
    out = None
    for h in range(LANES // NA_HEAD_DIM):
        mask = _head_mask(q.shape, h)
        qh = jnp.where(mask, q, 0.0).astype(BF16)
        s = lax.dot_general(qh, k, dn, preferred_element_type=F32)
        e = jnp.exp(s - jnp.max(s, axis=-1, keepdims=True))
        oh = _bdot(e.astype(BF16), v) * (1.0 / jnp.sum(e, axis=-1, keepdims=True))
        out = oh if out is None else jnp.where(mask, oh, out)
    o_ref[0] = out.astype(o_ref.dtype)


def _context_attention(proj_ctx):
    bsz, lc, _ = proj_ctx.shape
    nhp = D_BRANCH // LANES
    cq, ck, cv = COL_Q // LANES, COL_K // LANES, COL_V // LANES
    return pl.pallas_call(
        _ctx_attn_kernel,
        grid=(bsz, nhp),
        in_specs=[pl.BlockSpec((1, lc, LANES), lambda b, hp: (b, 0, cq + hp)),
                  pl.BlockSpec((1, lc, LANES), lambda b, hp: (b, 0, ck + hp)),
                  pl.BlockSpec((1, lc, LANES), lambda b, hp: (b, 0, cv + hp))],
        out_specs=pl.BlockSpec((1, lc, LANES), lambda b, hp: (b, 0, hp)),
        out_shape=jax.ShapeDtypeStruct((bsz, lc, D_BRANCH), BF16),
        compiler_params=_params("parallel", "parallel"),
        name="context_attention",
    )(proj_ctx, proj_ctx, proj_ctx)


def _to_grouped(x):
    n1 = x.shape[0] // FFT_N2
    ng = FFT_N2 // FFT_G
    x4 = x.reshape(n1, ng, FFT_G, x.shape[1])
    return [x4[:, g].reshape(n1 * FFT_G, x.shape[1]) for g in range(ng)]


def _from_grouped(ref, lead, rows):
    ng = FFT_N2 // FFT_G
    ntile = ref.shape[len(lead)]
    cols = []
    for t in range(ntile):
        parts = [ref[lead + (t, g, slice(i * FFT_G, (i + 1) * FFT_G), slice(None))]
                 for i in range(rows // FFT_N2) for g in range(ng)]
        cols.append(jnp.concatenate(parts, axis=0))
    return jnp.concatenate(cols, axis=1)


def _hy_pre_kernel(x0_ref, x1_ref, v_ref, w_ref, b_ref, x0_out, w_out, xs0, xs1, xs2, *, grouped):
    x0_out[0] = _dwconv(x0_ref, w_ref.at[0], b_ref.at[0], HY_CONV, xs0)
    w = (_dwconv(x1_ref, w_ref.at[1], b_ref.at[1], HY_CONV, xs1)
         * _dwconv(v_ref, w_ref.at[2], b_ref.at[2], HY_CONV, xs2))
    if grouped:
        for g, part in enumerate(_to_grouped(w)):
            w_out[0, 0, g] = part
    else:
        w_out[0] = w


def _hyena_pre(proj, conv_w, conv_b, grouped):
    bsz, length, _ = proj.shape
    nct = D_BRANCH // LANES
    c0 = COL_HY // LANES
    w3 = conv_w.reshape(HY_CONV, 3, D_BRANCH).transpose(1, 0, 2)
    b3 = conv_b.reshape(3, 1, D_BRANCH)
    spec = lambda seg: pl.BlockSpec((1, length, LANES), lambda b, c: (b, 0, c0 + seg * nct + c))
    nat_spec = pl.BlockSpec((1, length, LANES), lambda b, c: (b, 0, c))
    nat_shape = jax.ShapeDtypeStruct((bsz, length, D_BRANCH), F32)
    if grouped:
        ng = FFT_N2 // FFT_G
        w_spec = pl.BlockSpec((1, 1, ng, length // ng, LANES), lambda b, c: (b, c, 0, 0, 0))
        w_shape = jax.ShapeDtypeStruct((bsz, nct, ng, length // ng, LANES), F32)
    else:
        w_spec, w_shape = nat_spec, nat_shape
    return pl.pallas_call(
        functools.partial(_hy_pre_kernel, grouped=grouped),
        grid=(bsz, nct),
        in_specs=[spec(0), spec(1), spec(2),
                  pl.BlockSpec((3, HY_CONV, LANES), lambda b, c: (0, 0, c)),
                  pl.BlockSpec((3, 1, LANES), lambda b, c: (0, 0, c))],
        out_specs=[nat_spec, w_spec],
        out_shape=[nat_shape, w_shape],
        scratch_shapes=[_conv_scratch(length)] * 3,
        compiler_params=_params("parallel", "parallel"),
        name="hyena_pre",
    )(proj, proj, proj, w3, b3)


def _filter_kernel(z_ref, zg_ref, w1_ref, b1_ref, f1_ref, w2_ref, b2_ref, f2_ref, w3_ref, dl_ref,
                   *out_refs, grouped):
    def ffn(z):
        hid = jnp.sin(f1_ref[...] * (_dot3(z, w1_ref[...]) + b1_ref[...]))
        return jnp.sin(f2_ref[...] * (_dot3(hid, w2_ref[...]) + b2_ref[...]))

    z = z_ref[...]
    zg = zg_ref[...]
    tl = z.shape[0]
    h_fwd = _dot3(ffn(z), w3_ref[:, :D_BRANCH]) * jnp.exp(-z[:, 0:1] * dl_ref[...])
    row = pl.program_id(0) * tl + lax.broadcasted_iota(jnp.int32, (tl, D_BRANCH), 0)
    g = _dot3(ffn(zg), w3_ref[:, D_BRANCH:]) * jnp.exp(-zg[:, 0:1] * dl_ref[...])
    g = jnp.where(row == 0, 0.0, g)
    if grouped:
        for s, val in enumerate((h_fwd, g)):
            for t in range(D_BRANCH // LANES):
                for gi, part in enumerate(_to_grouped(val[:, t * LANES:(t + 1) * LANES])):
                    out_refs[0][s, t, gi] = part
    else:
        out_refs[0][...] = h_fwd
        out_refs[1][...] = g


def _hyena_filters(length, w1, b1, freq1, w2, b2, freq2, w3, grouped):
    t = jnp.arange(length, dtype=F32) / length
    ang = 2.0 * math.pi * t[:, None] * jnp.arange(1, HY_POS_BANDS + 1, dtype=F32)
    z = jnp.concatenate([t[:, None], jnp.cos(ang), jnp.sin(ang)], -1)
    z = jnp.pad(z, ((0, 0), (0, LANES - HY_EMB)))
    zg = jnp.concatenate([z[:1], jnp.flip(z[1:], 0)], 0)
    w1p = jnp.pad(w1, ((0, LANES - HY_EMB), (0, 0)))
    deltas = jnp.abs(jnp.linspace(math.log(HY_DECAY_TARGET) / HY_FAST_DECAY,
                                  math.log(HY_DECAY_TARGET) / HY_SLOW_DECAY, D_BRANCH, dtype=F32))
    tl = min(length, 512)
    full = lambda a: pl.BlockSpec(a.shape, lambda i: (0,) * a.ndim)
    args = [w1p, b1.reshape(1, -1), freq1.reshape(1, -1), w2, b2.reshape(1, -1), freq2.reshape(1, -1),
            w3, deltas.reshape(1, -1)]
    zspec = pl.BlockSpec((tl, LANES), lambda i: (i, 0))
    if grouped:
        nct, ng = D_BRANCH // LANES, FFT_N2 // FFT_G
        out_specs = [pl.BlockSpec((2, nct, ng, tl // ng, LANES), lambda i: (0, 0, 0, i, 0))]
        out_shape = [jax.ShapeDtypeStruct((2, nct, ng, length // ng, LANES), F32)]
    else:
        out_specs = [pl.BlockSpec((tl, D_BRANCH), lambda i: (i, 0))] * 2
        out_shape = [jax.ShapeDtypeStruct((length, D_BRANCH), F32)] * 2
    return pl.pallas_call(
        functools.partial(_filter_kernel, grouped=grouped),
        grid=(length // tl,),
        in_specs=[zspec, zspec] + [full(a) for a in args],
        out_specs=out_specs,
        out_shape=out_shape,
        compiler_params=_params("parallel"),
        name="hyena_filters",
    )(z, zg, *args)


def _fft_tables():
    n = FFT_N1 * FFT_N2
    half = FFT_N1 // 2
    k1 = np.arange(FFT_N1, dtype=np.float64)[:, None]
    n1 = np.arange(half, dtype=np.float64)[None, :]
    t1 = np.zeros((FFT_N2, 2 * FFT_N1, 2 * half))
    for n2 in range(FFT_N2):
        th = 2.0 * np.pi * k1 * (FFT_N2 * n1 + n2) / n
        gc, gs = np.cos(th), np.sin(th)
        t1[n2] = np.block([[gc, gs], [-gs, gc]])
    t1_inv = np.transpose(t1, (0, 2, 1)) / n
    sign = np.where(np.arange(2 * FFT_N1) % 2 == 0, 1.0, -1.0)[None, :, None]
    t1_filt = np.concatenate([t1[:, :, :half], sign * t1[:, :, :half]], axis=2)
    k2 = np.arange(FFT_N2, dtype=np.float64)
    th2 = 2.0 * np.pi * np.outer(k2, k2) / FFT_N2
    fc, fs = np.cos(th2), np.sin(th2)
    t2 = np.block([[fc, fs], [-fs, fc]])
    t2_inv = np.block([[fc, -fs], [fs, fc]])

    def hl(a):
        a32 = jnp.asarray(a, F32)
        hi = a32.astype(BF16)
        return hi, (a32 - hi.astype(F32)).astype(BF16)

    return hl(t1), hl(t1_inv), hl(t2), hl(t2_inv), hl(t1_filt)


TILES_PER_DOT = FFT_CT // LANES


def _lane_cat(parts):
    return parts[0] if len(parts) == 1 else jnp.concatenate(parts, axis=1)


def _fft1_kernel(z_ref, th_ref, tl_ref, a_ref):
    half = FFT_N1 // 2
    ntile = z_ref.shape[1]
    for j in range(FFT_G):
        rows = pl.ds(j, half, stride=FFT_G)
        for c in range(ntile // TILES_PER_DOT):
            tiles = range(c * TILES_PER_DOT, (c + 1) * TILES_PER_DOT)
            d = jnp.concatenate([_lane_cat([z_ref[s, t, 0, rows, :] for t in tiles]) for s in range(2)], axis=0)
            dh, dl = _split(d)
            a = _dot3_t(th_ref[j], tl_ref[j], dh, dl)
            for ri in range(2):
                for i, t in enumerate(tiles):
                    a_ref[0, 0, t, ri, pl.ds(j, FFT_N1, stride=FFT_G), :] = (
                        a[ri * FFT_N1:(ri + 1) * FFT_N1, i * LANES:(i + 1) * LANES])


def _fft_stage1(zg, t1):
    nseq, ntile, ng, rows, _ = zg.shape
    half = FFT_N1 // 2
    assert rows == half * FFT_G and ng * FFT_G == FFT_N2 and nseq % 2 == 0
    tspec = pl.BlockSpec((FFT_G, 2 * FFT_N1, 2 * half), lambda p, g: (g, 0, 0))
    return pl.pallas_call(
        _fft1_kernel,
        grid=(nseq // 2, ng),
        in_specs=[pl.BlockSpec((2, ntile, 1, rows, LANES), lambda p, g: (p, 0, g, 0, 0)), tspec, tspec],
        out_specs=pl.BlockSpec((1, 1, ntile, 2, FFT_G * FFT_N1, LANES), lambda p, g: (p, g, 0, 0, 0, 0)),
        out_shape=jax.ShapeDtypeStruct((nseq // 2, ng, ntile, 2, FFT_G * FFT_N1, LANES), F32),
        compiler_params=_params("parallel", "parallel"),
        name="fft_stage1",
    )(zg, *t1)


def _load_a(a_ref, k, tiles):
    ng = a_ref.shape[1]
    return jnp.concatenate(
        [_lane_cat([jnp.concatenate([a_ref[0, g, t, ri, k * FFT_G:(k + 1) * FFT_G, :] for g in range(ng)], axis=0)
                    for t in tiles]) for ri in range(2)], axis=0)


def _fft2_spec_kernel(a_ref, th_ref, tl_ref, k_ref):
    ntile = a_ref.shape[2]
    for k in range(FFT_KB):
        for c in range(ntile // TILES_PER_DOT):
            tiles = range(c * TILES_PER_DOT, (c + 1) * TILES_PER_DOT)
            dh, dl = _split(_load_a(a_ref, k, tiles))
            x = _dot3_t(th_ref[...], tl_ref[...], dh, dl)
            sl = slice(c * FFT_CT, (c + 1) * FFT_CT)
            k_ref[k, 0, :, sl] = x[:FFT_N2]
            k_ref[k, 1, :, sl] = x[FFT_N2:]


def _fft_spectrum(a_filt, t2):
    _, ng, ntile, _, _, _ = a_filt.shape
    ch = ntile * LANES
    tspec = pl.BlockSpec((2 * FFT_N2, 2 * FFT_N2), lambda i: (0, 0))
    return pl.pallas_call(
        _fft2_spec_kernel,
        grid=(FFT_N1 // FFT_KB,),
        in_specs=[pl.BlockSpec((1, ng, ntile, 2, FFT_KB * FFT_G, LANES), lambda i: (0, 0, 0, 0, i, 0)),
                  tspec, tspec],
        out_specs=pl.BlockSpec((FFT_KB, 2, FFT_N2, ch), lambda i: (i, 0, 0, 0)),
        out_shape=jax.ShapeDtypeStruct((FFT_N1, 2, FFT_N2, ch), F32),
        compiler_params=_params("parallel"),
        name="fft_spectrum",
    )(a_filt, *t2)


def _fft2_kernel(a_ref, k_ref, th_ref, tl_ref, ih_ref, il_ref, b_ref):
    ntile = a_ref.shape[2]
    for k in range(FFT_KB):
        for c in range(ntile // TILES_PER_DOT):
            tiles = range(c * TILES_PER_DOT, (c + 1) * TILES_PER_DOT)
            dh, dl = _split(_load_a(a_ref, k, tiles))
            x = _dot3_t(th_ref[...], tl_ref[...], dh, dl)
            xr, xi = x[:FFT_N2], x[FFT_N2:]
            sl = slice(c * FFT_CT, (c + 1) * FFT_CT)
            kr, ki = k_ref[k, 0, :, sl], k_ref[k, 1, :, sl]
            y = jnp.concatenate([xr * kr - xi * ki, xr * ki + xi * kr], axis=0)
            yh, yl = _split(y)
            out = _dot3_t(ih_ref[...], il_ref[...], yh, yl)
            for ri in range(2):
                for i, t in enumerate(tiles):
                    b_ref[0, 0, t, ri, pl.ds(k, FFT_N2, stride=FFT_KB), :] = (
                        out[ri * FFT_N2:(ri + 1) * FFT_N2, i * LANES:(i + 1) * LANES])


def _fft_stage2(a, kf, t2, t2_inv):
    npair, ng, ntile, _, _, _ = a.shape
    ch = ntile * LANES
    nkb = FFT_N1 // FFT_KB
    tspec = pl.BlockSpec((2 * FFT_N2, 2 * FFT_N2), lambda i, p: (0, 0))
    return pl.pallas_call(
        _fft2_kernel,
        grid=(nkb, npair),
        in_specs=[pl.BlockSpec((1, ng, ntile, 2, FFT_KB * FFT_G, LANES), lambda i, p: (p, 0, 0, 0, i, 0)),
                  pl.BlockSpec((FFT_KB, 2, FFT_N2, ch), lambda i, p: (i, 0, 0, 0)),
                  tspec, tspec, tspec, tspec],
        out_specs=pl.BlockSpec((1, 1, ntile, 2, FFT_KB * FFT_N2, LANES), lambda i, p: (p, i, 0, 0, 0, 0)),
        out_shape=jax.ShapeDtypeStruct((npair, nkb, ntile, 2, FFT_KB * FFT_N2, LANES), F32),
        compiler_params=_params("parallel", "parallel"),
        name="fft_stage2",
    )(a, kf, *t2, *t2_inv)


def _fft3_kernel(b_ref, th_ref, tl_ref, y_ref):
    half = FFT_N1 // 2
    nkb, ntile = b_ref.shape[1], b_ref.shape[2]
    for j in range(FFT_G):
        for c in range(ntile // TILES_PER_DOT):
            tiles = range(c * TILES_PER_DOT, (c + 1) * TILES_PER_DOT)
            d = jnp.concatenate(
                [_lane_cat([jnp.concatenate([b_ref[0, kb, t, ri, j * FFT_KB:(j + 1) * FFT_KB, :]
                                             for kb in range(nkb)], axis=0) for t in tiles])
                 for ri in range(2)], axis=0)
            dh, dl = _split(d)
            y = _dot3_t(th_ref[j], tl_ref[j], dh, dl)
            for s in range(2):
                for i, t in enumerate(tiles):
                    y_ref[s, t, 0, pl.ds(j, half, stride=FFT_G), :] = (
                        y[s * half:(s + 1) * half, i * LANES:(i + 1) * LANES])


def _fft_stage3(bm, t1_inv):
    npair, nkb, ntile, _, _, _ = bm.shape
    half = FFT_N1 // 2
    ng = FFT_N2 // FFT_G
    rows = half * FFT_G
    tspec = pl.BlockSpec((FFT_G, 2 * half, 2 * FFT_N1), lambda p, g: (g, 0, 0))
    return pl.pallas_call(
        _fft3_kernel,
        grid=(npair, ng),
        in_specs=[pl.BlockSpec((1, nkb, ntile, 2, FFT_KB * FFT_G, LANES), lambda p, g: (p, 0, 0, 0, g, 0)),
                  tspec, tspec],
        out_specs=pl.BlockSpec((2, ntile, 1, rows, LANES), lambda p, g: (p, 0, g, 0, 0)),
        out_shape=jax.ShapeDtypeStruct((2 * npair, ntile, ng, rows, LANES), F32),
        compiler_params=_params("parallel", "parallel"),
        name="fft_stage3",
    )(bm, *t1_inv)


def _hyena_long_conv(wg, filt_g, tables):
    t1, t1_inv, t2, t2_inv, t1_filt = tables
    kf = _fft_spectrum(_fft_stage1(filt_g, t1_filt), t2)
    a = _fft_stage1(wg, t1)
    bm = _fft_stage2(a, kf, t2, t2_inv)
    return _fft_stage3(bm, t1_inv)


def _dense_dft_tables(length):
    n = 2 * length
    k = np.arange(n, dtype=np.float64)
    th = 2.0 * np.pi * np.outer(k, k) / n
    c, s = np.cos(th), np.sin(th)
    t_full = np.concatenate([c, -s], 0)
    t_fwd = t_full[:, :length]
    t_inv = np.concatenate([c[:length], -s[:length]], 1) / n
    return tuple(jnp.asarray(a, F32) for a in (t_full, t_fwd, t_inv))


def _ctx_spec_kernel(t_ref, h_ref, o_ref):
    o_ref[...] = _dot3(t_ref[...], h_ref[...])


def _ctx_conv_kernel(w_ref, k_ref, tf_ref, ti_ref, o_ref):
    n = k_ref.shape[0] // 2
    x = _dot3(tf_ref[...], w_ref[0])
    xr, xi = x[:n], x[n:]
    kr, ki = k_ref[:n], k_ref[n:]
    y = jnp.concatenate([xr * kr - xi * ki, xr * ki + xi * kr], axis=0)
    o_ref[0] = _dot3(ti_ref[...], y)


def _ctx_long_conv(w, h_fwd, g):
    bsz, length, ch = w.shape
    n = 2 * length
    t_full, t_fwd, t_inv = _dense_dft_tables(length)
    kern = jnp.concatenate([h_fwd, g], 0)
    kf = pl.pallas_call(
        _ctx_spec_kernel,
        grid=(1,),
        in_specs=[pl.BlockSpec((2 * n, n), lambda i: (0, 0)), pl.BlockSpec((n, ch), lambda i: (0, 0))],
        out_specs=pl.BlockSpec((2 * n, ch), lambda i: (0, 0)),
        out_shape=jax.ShapeDtypeStruct((2 * n, ch), F32),
        compiler_params=_params("arbitrary"),
        name="ctx_spectrum",
    )(t_full, kern)
    return pl.pallas_call(
        _ctx_conv_kernel,
        grid=(bsz,),
        in_specs=[pl.BlockSpec((1, length, ch), lambda b: (b, 0, 0)),
                  pl.BlockSpec((2 * n, ch), lambda b: (0, 0)),
                  pl.BlockSpec((2 * n, length), lambda b: (0, 0)),
                  pl.BlockSpec((length, 2 * n), lambda b: (0, 0))],
        out_specs=pl.BlockSpec((1, length, ch), lambda b: (b, 0, 0)),
        out_shape=jax.ShapeDtypeStruct((bsz, length, ch), F32),
        compiler_params=_params("parallel"),
        name="ctx_long_conv",
    )(w, kf, t_fwd, t_inv)


def _merge_kernel(hf_ref, hb_ref, yb_ref, cv_ref, w_ref, x0_ref, ag_ref, bg_ref, cg_ref,
                  ga_ref, gb_ref, gc_ref, h_ref, gate_ref, wb_ref, wo_ref, ds_ref, lg_ref, lb_ref,
                  o_ref, *, alpha, grouped):
    ya = hf_ref[0].astype(F32) + hb_ref[0].astype(F32)
    if grouped:
        tm = o_ref.shape[1]
        w = _from_grouped(w_ref, (0,), tm)
        cv = _from_grouped(cv_ref, (0,), tm)
    else:
        w = w_ref[0]
        cv = cv_ref[0]
    yc = x0_ref[0] * (cv + ds_ref[...] * w)

    def gated(y, u_ref):
        u = u_ref[0].astype(F32)
        return ((y * u) * (1.0 + jnp.tanh(u))).astype(BF16)

    def sig2(u_ref):
        return 1.0 + jnp.tanh(u_ref[0].astype(F32))

    pa = _bdot(gated(ya, ag_ref), wb_ref[0])
    pb = _bdot(gated(yb_ref[0].astype(F32), bg_ref), wb_ref[1])
    pc = _bdot(gated(yc, cg_ref), wb_ref[2])
    m = sig2(ga_ref) * pa + sig2(gb_ref) * pb + sig2(gc_ref) * pc
    out = _bdot(m.astype(BF16), wo_ref[...])
    z = alpha * h_ref[0] + gate_ref[0] * out
    o_ref[0] = _layer_norm(z) * lg_ref[...] + lb_ref[...]


def _merge(proj, hf, hb, yb, conv, w, x0c, h, gate, wb_bf16, wo_bf16, d_skip, ln_g, ln_b, alpha, grouped):
    bsz, length, d = h.shape
    tm = min(length, 512)
    half = lambda: pl.BlockSpec((1, tm, D_BRANCH), lambda b, i: (b, i, 0))
    if grouped:
        ng = FFT_N2 // FFT_G
        hy = lambda: pl.BlockSpec((1, D_BRANCH // LANES, ng, tm // ng, LANES), lambda b, i: (b, 0, 0, i, 0))
    else:
        hy = half
    pcol = lambda col, width: pl.BlockSpec((1, tm, width), lambda b, i: (b, i, col // width))
    row = lambda n: pl.BlockSpec((1, n), lambda b, i: (0, 0))
    return pl.pallas_call(
        functools.partial(_merge_kernel, alpha=alpha, grouped=grouped),
        grid=(bsz, length // tm),
        in_specs=[half(), half(), half(), hy(), hy(), half(),
                  pcol(COL_LRU_G, D_BRANCH), pcol(COL_NA_G, D_BRANCH), pcol(COL_HY_G, D_BRANCH),
                  pcol(COL_GA, D_MODEL), pcol(COL_GB, D_MODEL), pcol(COL_GC, D_MODEL),
                  pl.BlockSpec((1, tm, d), lambda b, i: (b, i, 0)),
                  pl.BlockSpec((1, 1, d), lambda b, i: (b, 0, 0)),
                  pl.BlockSpec((3, D_BRANCH, d), lambda b, i: (0, 0, 0)),
                  pl.BlockSpec((d, d), lambda b, i: (0, 0)),
                  row(D_BRANCH), row(d), row(d)],
        out_specs=pl.BlockSpec((1, tm, d), lambda b, i: (b, i, 0)),
        out_shape=jax.ShapeDtypeStruct((bsz, length, d), F32),
        compiler_params=_params("parallel", "parallel"),
        name="merge",
    )(hf, hb, yb, conv, w, x0c, proj, proj, proj, proj, proj, proj, h, gate,
      wb_bf16, wo_bf16, d_skip.reshape(1, -1), ln_g.reshape(1, -1), ln_b.reshape(1, -1))


def _block_diag(w):
    g, n, _ = w.shape
    eye = jnp.eye(g, dtype=w.dtype)
    return (eye[:, None, :, None] * w[:, :, None, :]).reshape(g * n, g * n)


def kernel(x, c, ctx, c_ctx, w_mod, b_mod, w_in, lru_conv_w, lru_conv_b, lru_w_a, lru_b_a, lru_w_x, lru_b_x,
           lru_lambda, na_rpb, hy_conv_w, hy_conv_b, hy_w1, hy_b1, hy_freq1, hy_w2, hy_b2, hy_freq2, hy_w3,
           hy_d_skip, w_branch_out, w_out, ln_g, ln_b):
    depth = w_in.shape[0]
    bsz, seq_len, d = x.shape
    ctx_len = ctx.shape[1]
    alpha = (2 * depth) ** 0.25
    tables = _fft_tables()
    pad_rows = (-(bsz + 1)) % 8
    c_rows = jnp.concatenate([c, c_ctx[None, :], jnp.zeros((pad_rows, d), F32)], 0)

    col = np.arange(D_IN)
    is_gate = ((col >= COL_LRU_G) & (col < COL_Q)) | ((col >= COL_NA_G) & (col < COL_HY)) | (col >= COL_HY_G)
    gate_col_scale = jnp.asarray(np.where(is_gate, 0.5, 1.0)[None, :], F32)

    h, hc = x, ctx
    for l in range(depth):
        last = l == depth - 1
        mod = _modulation(c_rows, w_mod[l], b_mod[l])
        shift, scale, gate = (mod[:bsz, None, i * d:(i + 1) * d] for i in range(3))
        shift_x, scale_x, gate_x = (jnp.broadcast_to(mod[bsz, i * d:(i + 1) * d], (bsz, 1, d)) for i in range(3))
        gate, gate_x = 0.5 * gate, 0.5 * gate_x
        w_in_l = (w_in[l] * gate_col_scale).astype(BF16)
        proj = _in_projection(h, scale, shift, w_in_l)
        proj_x = _in_projection(hc, scale_x, shift_x, w_in_l)

        w_cat = jnp.stack([jnp.concatenate([_block_diag(lru_w_a[l, dd]), _block_diag(lru_w_x[l, dd])], 1)
                           for dd in range(2)], 0)
        w_cat = w_cat.astype(BF16)
        b_a = 0.5 * lru_b_a[l][:, None, :]
        b_x = 0.5 * lru_b_x[l][:, None, :]
        sp = (-0.5 * LRU_C) * jax.nn.softplus(-lru_lambda[l])[:, None, :]
        xa = _lru_conv(proj, 0.5 * lru_conv_w[l], 0.5 * lru_conv_b[l])
        xa_x = _lru_conv(proj_x, 0.5 * lru_conv_w[l], 0.5 * lru_conv_b[l])
        hf_x, hb_x, h0 = _rglru(xa_x, w_cat, b_a, b_x, sp, jnp.zeros((2, bsz, D_BRANCH), F32))
        hf, hb, _ = _rglru(xa, w_cat, b_a, b_x, sp, h0)

        yb = _neighbourhood_attention(proj, proj_x, _na_bias(na_rpb[l], seq_len // GRID_W))

        filt_args = (hy_w1[l], hy_b1[l], hy_freq1[l], hy_w2[l], hy_b2[l], hy_freq2[l], hy_w3[l])
        x0c, w = _hyena_pre(proj, hy_conv_w[l], hy_conv_b[l], grouped=True)
        (filt_g,) = _hyena_filters(seq_len, *filt_args, grouped=True)
        conv = _hyena_long_conv(w, filt_g, tables)

        wb = w_branch_out[l].astype(BF16)
        wo = w_out[l].astype(BF16)
        h_new = _merge(proj, hf, hb, yb, conv, w, x0c, h, gate, wb, wo, hy_d_skip[l], ln_g[l], ln_b[l], alpha,
                       grouped=True)

        if not last:
            yb_x = _context_attention(proj_x)
            x0c_x, w_x = _hyena_pre(proj_x, hy_conv_w[l], hy_conv_b[l], grouped=False)
            hx_fwd, hx_g = _hyena_filters(ctx_len, *filt_args, grouped=False)
            conv_x = _ctx_long_conv(w_x, hx_fwd, hx_g)
            hc = _merge(proj_x, hf_x, hb_x, yb_x, conv_x, w_x, x0c_x, hc, gate_x, wb, wo, hy_d_skip[l],
                        ln_g[l], ln_b[l], alpha, grouped=False)
        h = h_new
    return h
```

```python
import functools
import math

import numpy as np
import jax
import jax.numpy as jnp
from jax import lax
from jax.experimental import pallas as pl
from jax.experimental.pallas import tpu as pltpu

F32 = jnp.float32
BF16 = jnp.bfloat16

D_MODEL = 1024
D_BRANCH = 512
N_LRU_BLOCKS = 8
LRU_CONV = 4
LRU_C = 8.0
NA_HEAD_DIM = 64
NA_HEADS = 8
NA_KH = 8
NA_KW = 16
NA_RPB = 8
NA_BAND = 16
NA_KBLK = 256
GRID_W = 64
HY_CONV = 3
HY_POS_BANDS = 16
HY_EMB = 1 + 2 * HY_POS_BANDS
HY_HID = 64
HY_DECAY_TARGET = 1e-2
HY_FAST_DECAY = 0.3
HY_SLOW_DECAY = 1.5
LN_EPS = 1e-5
D_IN = 8192

COL_LRU_X, COL_LRU_G = 0, 512
COL_Q, COL_K, COL_V, COL_NA_G = 1024, 1536, 2048, 2560
COL_HY, COL_HY_G = 3072, 4608
COL_GA, COL_GB, COL_GC = 5120, 6144, 7168

LANES = 128
VMEM_LIMIT = 56 * 1024 * 1024
MASK_VALUE = -1e30

FFT_N1 = 128
FFT_N2 = 64
FFT_G = 8
FFT_KB = 8
FFT_CT = 256


def _params(*sem):
    return pltpu.CompilerParams(dimension_semantics=sem, vmem_limit_bytes=VMEM_LIMIT)


def _split(x):
    hi = x.astype(BF16)
    lo = (x - hi.astype(F32)).astype(BF16)
    return hi, lo


def _bdot(a, b):
    return jnp.dot(a, b, preferred_element_type=F32)


def _dot3(a, b):
    ah, al = _split(a)
    bh, bl = _split(b)
    return _bdot(ah, bh) + _bdot(ah, bl) + _bdot(al, bh)


def _dot3_t(th, tl, dh, dl):
    return _bdot(th, dh) + _bdot(th, dl) + _bdot(tl, dh)


def _sigmoid(x):
    return 0.5 * jnp.tanh(0.5 * x) + 0.5


def _silu(x):
    return x * _sigmoid(x)


def _layer_norm(x):
    mu = jnp.mean(x, axis=-1, keepdims=True)
    xc = x - mu
    var = jnp.mean(xc * xc, axis=-1, keepdims=True)
    return xc * lax.rsqrt(var + LN_EPS)


def _mod_kernel(c_ref, w_ref, b_ref, o_ref):
    o_ref[...] = _dot3(_silu(c_ref[...]), w_ref[...]) + b_ref[...]


def _modulation(c_rows, w_mod, b_mod):
    rows, d = c_rows.shape
    n = w_mod.shape[1]
    tn = 512
    return pl.pallas_call(
        _mod_kernel,
        grid=(n // tn,),
        in_specs=[pl.BlockSpec((rows, d), lambda j: (0, 0)),
                  pl.BlockSpec((d, tn), lambda j: (0, j)),
                  pl.BlockSpec((1, tn), lambda j: (0, j))],
        out_specs=pl.BlockSpec((rows, tn), lambda j: (0, j)),
        out_shape=jax.ShapeDtypeStruct((rows, n), F32),
        compiler_params=_params("parallel"),
        name="modulation",
    )(c_rows, w_mod, b_mod.reshape(1, n))


def _proj_kernel(x_ref, sc_ref, sh_ref, w_ref, o_ref, u_scr):
    @pl.when(pl.program_id(2) == 0)
    def _():
        u = _layer_norm(x_ref[0]) * (1.0 + sc_ref[0]) + sh_ref[0]
        u_scr[...] = u.astype(BF16)

    o_ref[0] = _bdot(u_scr[...], w_ref[...]).astype(o_ref.dtype)


def _in_projection(x, scale, shift, w_bf16):
    bsz, length, d = x.shape
    n = w_bf16.shape[1]
    tm = min(length, 2048)
    tn = 2048
    return pl.pallas_call(
        _proj_kernel,
        grid=(bsz, length // tm, n // tn),
        in_specs=[pl.BlockSpec((1, tm, d), lambda b, i, j: (b, i, 0)),
                  pl.BlockSpec((1, 1, d), lambda b, i, j: (b, 0, 0)),
                  pl.BlockSpec((1, 1, d), lambda b, i, j: (b, 0, 0)),
                  pl.BlockSpec((d, tn), lambda b, i, j: (0, j))],
        out_specs=pl.BlockSpec((1, tm, tn), lambda b, i, j: (b, i, j)),
        out_shape=jax.ShapeDtypeStruct((bsz, length, n), BF16),
        scratch_shapes=[pltpu.VMEM((tm, d), BF16)],
        compiler_params=_params("parallel", "parallel", "arbitrary"),
        name="in_projection",
    )(x, scale, shift, w_bf16)


PAD = 8


def _dwconv(x_ref, w_ref, b_ref, kw, xs):
    length = x_ref.shape[1]
    left = kw // 2
    assert left <= PAD and kw - 1 - left <= PAD
    zeros = jnp.zeros((PAD, xs.shape[1]), F32)
    xs[:PAD, :] = zeros
    xs[PAD + length:, :] = zeros
    xs[PAD:PAD + length, :] = x_ref[0].astype(F32)
    acc = b_ref[...] + xs[pl.ds(PAD - left, length), :] * w_ref[0:1, :]
    for k in range(1, kw):
        acc = acc + xs[pl.ds(PAD - left + k, length), :] * w_ref[k:k + 1, :]
    return acc


def _conv_scratch(length):
    return pltpu.VMEM((length + 2 * PAD, LANES), F32)


def _lru_conv_kernel(x_ref, w_ref, b_ref, o_ref, xs):
    o_ref[0] = _dwconv(x_ref, w_ref, b_ref, LRU_CONV, xs)


def _lru_conv(proj, w, b):
    bsz, length, _ = proj.shape
    nct = D_BRANCH // LANES
    c0 = COL_LRU_X // LANES
    return pl.pallas_call(
        _lru_conv_kernel,
        grid=(bsz, nct),
        in_specs=[pl.BlockSpec((1, length, LANES), lambda b, c: (b, 0, c0 + c)),
                  pl.BlockSpec((LRU_CONV, LANES), lambda b, c: (0, c)),
                  pl.BlockSpec((1, LANES), lambda b, c: (0, c))],
        out_specs=pl.BlockSpec((1, length, LANES), lambda b, c: (b, 0, c)),
        out_shape=jax.ShapeDtypeStruct((bsz, length, D_BRANCH), F32),
        scratch_shapes=[_conv_scratch(length)],
        compiler_params=_params("parallel", "parallel"),
        name="lru_conv",
    )(proj, w, b.reshape(1, -1))


def _lru_kernel(xf_ref, xb_ref, w_ref, ba_ref, bx_ref, sp_ref, h0_ref, of_ref, ob_ref, hl_ref,
                a0, a1, b0, b1, o0, o1, h_scr, *, tt, stride):
    nb = xf_ref.shape[0]
    ch = xf_ref.shape[2]
    nct = ch // LANES

    @pl.when(pl.program_id(0) == 0)
    def _():
        h_scr[...] = h0_ref[...]

    for d, x_ref, a_scr, b_scr in ((0, xf_ref, a0, b0), (1, xb_ref, a1, b1)):
        for b in range(nb):
            xh = x_ref[b]
            pre = _bdot(xh.astype(BF16), w_ref[d])
            ta = jnp.tanh(pre[:, :ch] + ba_ref[d])
            tx = jnp.tanh(pre[:, ch:] + bx_ref[d])
            a = jnp.exp(sp_ref[d] * (1.0 + ta))
            q = 1.0 - a * a
            root = jnp.where(q > 0.0, q * lax.rsqrt(q), 0.0)
            bb = root * ((1.0 + tx) * xh)
            for ci in range(nct):
                sl = slice(ci * LANES, (ci + 1) * LANES)
                a_scr[ci, pl.ds(b * stride, tt), :] = a[:, sl]
                b_scr[ci, pl.ds(b * stride, tt), :] = bb[:, sl]

    def step(t, carry):
        tb = tt - 1 - t
        new = []
        for ci in range(nct):
            hf = a0[ci, pl.ds(t, nb, stride=stride), :] * carry[ci] + b0[ci, pl.ds(t, nb, stride=stride), :]
            o0[ci, pl.ds(t, nb, stride=stride), :] = hf
            new.append(hf)
        for ci in range(nct):
            hb = a1[ci, pl.ds(tb, nb, stride=stride), :] * carry[nct + ci] + b1[ci, pl.ds(tb, nb, stride=stride), :]
            o1[ci, pl.ds(tb, nb, stride=stride), :] = hb
            new.append(hb)
        return tuple(new)

    init = tuple(h_scr[d, :, ci * LANES:(ci + 1) * LANES] for d in range(2) for ci in range(nct))
    fin = lax.fori_loop(0, tt, step, init)
    for d in range(2):
        for ci in range(nct):
            h_scr[d, :, ci * LANES:(ci + 1) * LANES] = fin[d * nct + ci]
    hl_ref[...] = h_scr[...]
    for b in range(nb):
        for ci in range(nct):
            sl = slice(ci * LANES, (ci + 1) * LANES)
            of_ref[b, :, sl] = o0[ci, pl.ds(b * stride, tt), :].astype(of_ref.dtype)
            ob_ref[b, :, sl] = o1[ci, pl.ds(b * stride, tt), :].astype(ob_ref.dtype)


def _rglru(xa, w_cat, b_a, b_x, sp, h0):
    bsz, length, ch = xa.shape
    tt = 128
    stride = tt + 8
    nt = length // tt
    kern = functools.partial(_lru_kernel, tt=tt, stride=stride)
    scr = pltpu.VMEM((ch // LANES, bsz * stride, LANES), F32)
    return pl.pallas_call(
        kern,
        grid=(nt,),
        in_specs=[pl.BlockSpec((bsz, tt, ch), lambda i: (0, i, 0)),
                  pl.BlockSpec((bsz, tt, ch), lambda i: (0, nt - 1 - i, 0)),
                  pl.BlockSpec((2, ch, 2 * ch), lambda i: (0, 0, 0)),
                  pl.BlockSpec((2, 1, ch), lambda i: (0, 0, 0)),
                  pl.BlockSpec((2, 1, ch), lambda i: (0, 0, 0)),
                  pl.BlockSpec((2, 1, ch), lambda i: (0, 0, 0)),
                  pl.BlockSpec((2, bsz, ch), lambda i: (0, 0, 0))],
        out_specs=[pl.BlockSpec((bsz, tt, ch), lambda i: (0, i, 0)),
                   pl.BlockSpec((bsz, tt, ch), lambda i: (0, nt - 1 - i, 0)),
                   pl.BlockSpec((2, bsz, ch), lambda i: (0, 0, 0))],
        out_shape=[jax.ShapeDtypeStruct((bsz, length, ch), BF16)] * 2
                  + [jax.ShapeDtypeStruct((2, bsz, ch), F32)],
        scratch_shapes=[scr, scr, scr, scr, scr, scr, pltpu.VMEM((2, bsz, ch), F32)],
        compiler_params=_params("arbitrary"),
        name="rglru",
    )(xa, xa, w_cat, b_a, b_x, sp, h0)


def _head_mask(shape, h):
    lane = lax.broadcasted_iota(jnp.int32, shape, len(shape) - 1)
    return (lane >= h * NA_HEAD_DIM) & (lane < (h + 1) * NA_HEAD_DIM)


def _na_head(qh, k_band, v_band, bias_ref, h, kc, vc, s_scr):
    dn = (((1,), (1,)), ((), ()))
    blocks = [(k_band[o:o + NA_KBLK], v_band[o:o + NA_KBLK], o) for o in range(0, k_band.shape[0], NA_KBLK)]
    blocks += [(kc[o:o + NA_KBLK], vc[o:o + NA_KBLK], None) for o in range(0, kc.shape[0], NA_KBLK)]
    m = None
    for i, (kb, _, off) in enumerate(blocks):
        s = lax.dot_general(qh, kb, dn, preferred_element_type=F32)
        if off is not None:
            s = s + bias_ref[0, h, :, off:off + NA_KBLK]
        s_scr[:, i * NA_KBLK:(i + 1) * NA_KBLK] = s
        bm = jnp.max(s, axis=-1, keepdims=True)
        m = bm if m is None else jnp.maximum(m, bm)
    den = acc = None
    for i, (_, vb, _) in enumerate(blocks):
        e = jnp.exp(s_scr[:, i * NA_KBLK:(i + 1) * NA_KBLK] - m)
        bs = jnp.sum(e, axis=-1, keepdims=True)
        pv = _bdot(e.astype(BF16), vb)
        den = bs if den is None else den + bs
        acc = pv if acc is None else acc + pv
    return acc * (1.0 / den)


def _na_kernel(q_ref, k_ref, v_ref, kc_ref, vc_ref, bias_ref, o_ref, s_scr, *, rows, rpb, band):
    rb = pl.program_id(2)
    kh = min(NA_KH, rows)
    u0 = jnp.clip(rb * rpb - kh // 2, 0, rows - band)
    start = pl.multiple_of(u0 * GRID_W, GRID_W)
    k_band = k_ref[0, pl.ds(start, band * GRID_W), :]
    v_band = v_ref[0, pl.ds(start, band * GRID_W), :]
    q = q_ref[0].astype(F32) * NA_HEAD_DIM ** -0.5
    out = None
    for h in range(LANES // NA_HEAD_DIM):
        mask = _head_mask(q.shape, h)
        qh = jnp.where(mask, q, 0.0).astype(BF16)
        oh = _na_head(qh, k_band, v_band, bias_ref, h, kc_ref[0], vc_ref[0], s_scr)
        out = oh if out is None else jnp.where(mask, oh, out)
    o_ref[0] = out.astype(o_ref.dtype)


def _na_bias_kernel(r_ref, o_ref, *, tile, n_dr):
    wq = lax.broadcasted_iota(jnp.int32, (GRID_W, LANES), 0)
    lane = lax.broadcasted_iota(jnp.int32, (GRID_W, LANES), 1)
    col_start = jnp.clip(wq - NA_KW // 2, 0, GRID_W - NA_KW)
    masked = jnp.full((GRID_W, LANES), MASK_VALUE, F32)

    def toeplitz(dr, half):
        if dr == n_dr:
            return masked
        row = jnp.broadcast_to(r_ref[0, dr:dr + 1, :], (GRID_W, LANES))
        y = pltpu.roll(row, (half * GRID_W - (NA_KW - 1)) % LANES, axis=1, stride=1, stride_axis=0)
        wk = lane - half * GRID_W
        return jnp.where((wk >= col_start) & (wk < col_start + NA_KW), y, MASK_VALUE)

    nv, rpb, band = tile.shape
    for v in range(nv):
        for i in range(rpb):
            for a in range(band // 2):
                both = jnp.where(lane < GRID_W, toeplitz(int(tile[v, i, 2 * a]), 0),
                                 toeplitz(int(tile[v, i, 2 * a + 1]), 1))
                o_ref[v, 0, i * GRID_W:(i + 1) * GRID_W, a * LANES:(a + 1) * LANES] = both


def _na_bias(rpb, rows):
    n_dr, n_dc = 2 * NA_KH - 1, 2 * NA_KW - 1
    nrb = rows // NA_RPB
    band = NA_BAND
    assert 2 * GRID_W == LANES and band % 2 == 0
    assert nrb >= 3 and NA_RPB >= NA_KH // 2 and rows - band >= (nrb - 2) * NA_RPB - NA_KH // 2
    tile = np.full((3, NA_RPB, band), n_dr, np.int32)
    for v, rb in enumerate((0, 1, nrb - 1)):
        u0 = int(np.clip(rb * NA_RPB - NA_KH // 2, 0, rows - band))
        for i in range(NA_RPB):
            r = rb * NA_RPB + i
            r0 = int(np.clip(r - NA_KH // 2, 0, rows - NA_KH))
            for ku in range(band):
                if r0 <= u0 + ku < r0 + NA_KH:
                    tile[v, i, ku] = u0 + ku - r + NA_KH - 1
    r_pad = jnp.pad(rpb, ((0, 0), (0, 2 * NA_KH - n_dr), (0, LANES - n_dc)))
    tq, tk = NA_RPB * GRID_W, band * GRID_W
    return pl.pallas_call(
        functools.partial(_na_bias_kernel, tile=tile, n_dr=n_dr),
        grid=(NA_HEADS,),
        in_specs=[pl.BlockSpec((1, 2 * NA_KH, LANES), lambda h: (h, 0, 0))],
        out_specs=pl.BlockSpec((3, 1, tq, tk), lambda h: (0, h, 0, 0)),
        out_shape=jax.ShapeDtypeStruct((3, NA_HEADS, tq, tk), F32),
        compiler_params=_params("parallel"),
        name="na_bias",
    )(r_pad)


def _neighbourhood_attention(proj, proj_ctx, bias):
    bsz, length, _ = proj.shape
    lc = proj_ctx.shape[1]
    rows = length // GRID_W
    kh = min(NA_KH, rows)
    nhp = D_BRANCH // LANES
    hpb = LANES // NA_HEAD_DIM
    cq, ck, cv = COL_Q // LANES, COL_K // LANES, COL_V // LANES
    rpb = NA_RPB
    nrb = rows // rpb
    tq = rpb * GRID_W
    band = NA_BAND
    assert band >= rpb + kh - 1 and (band * GRID_W) % NA_KBLK == 0 and lc % NA_KBLK == 0

    def bias_map(b, hp, r):
        return (jnp.where(r == 0, 0, jnp.where(r == nrb - 1, 2, 1)), hp, 0, 0)

    return pl.pallas_call(
        functools.partial(_na_kernel, rows=rows, rpb=rpb, band=band),
        grid=(bsz, nhp, nrb),
        in_specs=[pl.BlockSpec((1, tq, LANES), lambda b, hp, r: (b, r, cq + hp)),
                  pl.BlockSpec((1, length, LANES), lambda b, hp, r: (b, 0, ck + hp)),
                  pl.BlockSpec((1, length, LANES), lambda b, hp, r: (b, 0, cv + hp)),
                  pl.BlockSpec((1, lc, LANES), lambda b, hp, r: (b, 0, ck + hp)),
                  pl.BlockSpec((1, lc, LANES), lambda b, hp, r: (b, 0, cv + hp)),
                  pl.BlockSpec((1, hpb, tq, band * GRID_W), bias_map)],
        out_specs=pl.BlockSpec((1, tq, LANES), lambda b, hp, r: (b, r, hp)),
        out_shape=jax.ShapeDtypeStruct((bsz, length, D_BRANCH), BF16),
        scratch_shapes=[pltpu.VMEM((tq, band * GRID_W + lc), F32)],
        compiler_params=_params("parallel", "parallel", "parallel"),
        name="neighbourhood_attention",
    )(proj, proj, proj, proj_ctx, proj_ctx, bias)


def _ctx_attn_kernel(q_ref, k_ref, v_ref, o_ref):
    q = q_ref[0].astype(F32) * NA_HEAD_DIM ** -0.5
    k = k_ref[0]
    v = v_ref[0]
    dn = (((1,), (1,)), ((), ()))
    out = None
    for h in range(LANES // NA_HEAD_DIM):
        mask = _head_mask(q.shape, h)
        qh = jnp.where(mask, q, 0.0).astype(BF16)
        s = lax.dot_general(qh, k, dn, preferred_element_type=F32)
        e = jnp.exp(s - jnp.max(s, axis=-1, keepdims=True))
        oh = _bdot(e.astype(BF16), v) * (1.0 / jnp.sum(e, axis=-1, keepdims=True))
        out = oh if out is None else jnp.where(mask, oh, out)
    o_ref[0] = out.astype(o_ref.dtype)


def _context_attention(proj_ctx):
    bsz, lc, _ = proj_ctx.shape
    nhp = D_BRANCH // LANES
    cq, ck, cv = COL_Q // LANES, COL_K // LANES, COL_V // LANES
    return pl.pallas_call(
        _ctx_attn_kernel,
        grid=(bsz, nhp),
        in_specs=[pl.BlockSpec((1, lc, LANES), lambda b, hp: (b, 0, cq + hp)),
                  pl.BlockSpec((1, lc, LANES), lambda b, hp: (b, 0, ck + hp)),
                  pl.BlockSpec((1, lc, LANES), lambda b, hp: (b, 0, cv + hp))],
        out_specs=pl.BlockSpec((1, lc, LANES), lambda b, hp: (b, 0, hp)),
        out_shape=jax.ShapeDtypeStruct((bsz, lc, D_BRANCH), BF16),
        compiler_params=_params("parallel", "parallel"),
        name="context_attention",
    )(proj_ctx, proj_ctx, proj_ctx)


def _to_grouped(x):
    n1 = x.shape[0] // FFT_N2
    ng = FFT_N2 // FFT_G
    x4 = x.reshape(n1, ng, FFT_G, x.shape[1])
    return [x4[:, g].reshape(n1 * FFT_G, x.shape[1]) for g in range(ng)]


def _from_grouped(ref, lead, rows):
    ng = FFT_N2 // FFT_G
    ntile = ref.shape[len(lead)]
    cols = []
    for t in range(ntile):
        parts = [ref[lead + (t, g, slice(i * FFT_G, (i + 1) * FFT_G), slice(None))]
                 for i in range(rows // FFT_N2) for g in range(ng)]
        cols.append(jnp.concatenate(parts, axis=0))
    return jnp.concatenate(cols, axis=1)


def _hy_pre_kernel(x0_ref, x1_ref, v_ref, w_ref, b_ref, x0_out, w_out, xs0, xs1, xs2, *, grouped):
    x0_out[0] = _dwconv(x0_ref, w_ref.at[0], b_ref.at[0], HY_CONV, xs0)
    w = (_dwconv(x1_ref, w_ref.at[1], b_ref.at[1], HY_CONV, xs1)
         * _dwconv(v_ref, w_ref.at[2], b_ref.at[2], HY_CONV, xs2))
    if grouped:
        for g, part in enumerate(_to_grouped(w)):
            w_out[0, 0, g] = part
    else:
        w_out[0] = w


def _hyena_pre(proj, conv_w, conv_b, grouped):
    bsz, length, _ = proj.shape
    nct = D_BRANCH // LANES
    c0 = COL_HY // LANES
    w3 = conv_w.reshape(HY_CONV, 3, D_BRANCH).transpose(1, 0, 2)
    b3 = conv_b.reshape(3, 1, D_BRANCH)
    spec = lambda seg: pl.BlockSpec((1, length, LANES), lambda b, c: (b, 0, c0 + seg * nct + c))
    nat_spec = pl.BlockSpec((1, length, LANES), lambda b, c: (b, 0, c))
    nat_shape = jax.ShapeDtypeStruct((bsz, length, D_BRANCH), F32)
    if grouped:
        ng = FFT_N2 // FFT_G
        w_spec = pl.BlockSpec((1, 1, ng, length // ng, LANES), lambda b, c: (b, c, 0, 0, 0))
        w_shape = jax.ShapeDtypeStruct((bsz, nct, ng, length // ng, LANES), F32)
    else:
        w_spec, w_shape = nat_spec, nat_shape
    return pl.pallas_call(
        functools.partial(_hy_pre_kernel, grouped=grouped),
        grid=(bsz, nct),
        in_specs=[spec(0), spec(1), spec(2),
                  pl.BlockSpec((3, HY_CONV, LANES), lambda b, c: (0, 0, c)),
                  pl.BlockSpec((3, 1, LANES), lambda b, c: (0, 0, c))],
        out_specs=[nat_spec, w_spec],
        out_shape=[nat_shape, w_shape],
        scratch_shapes=[_conv_scratch(length)] * 3,
        compiler_params=_params("parallel", "parallel"),
        name="hyena_pre",
    )(proj, proj, proj, w3, b3)


def _filter_kernel(z_ref, zg_ref, w1_ref, b1_ref, f1_ref, w2_ref, b2_ref, f2_ref, w3_ref, dl_ref,
                   *out_refs, grouped):
    def ffn(z):
        hid = jnp.sin(f1_ref[...] * (_dot3(z, w1_ref[...]) + b1_ref[...]))
        return jnp.sin(f2_ref[...] * (_dot3(hid, w2_ref[...]) + b2_ref[...]))

    z = z_ref[...]
    zg = zg_ref[...]
    tl = z.shape[0]
    h_fwd = _dot3(ffn(z), w3_ref[:, :D_BRANCH]) * jnp.exp(-z[:, 0:1] * dl_ref[...])
    row = pl.program_id(0) * tl + lax.broadcasted_iota(jnp.int32, (tl, D_BRANCH), 0)
    g = _dot3(ffn(zg), w3_ref[:, D_BRANCH:]) * jnp.exp(-zg[:, 0:1] * dl_ref[...])
    g = jnp.where(row == 0, 0.0, g)
    if grouped:
        for s, val in enumerate((h_fwd, g)):
            for t in range(D_BRANCH // LANES):
                for gi, part in enumerate(_to_grouped(val[:, t * LANES:(t + 1) * LANES])):
                    out_refs[0][s, t, gi] = part
    else:
        out_refs[0][...] = h_fwd
        out_refs[1][...] = g


def _hyena_filters(length, w1, b1, freq1, w2, b2, freq2, w3, grouped):
    t = jnp.arange(length, dtype=F32) / length
    ang = 2.0 * math.pi * t[:, None] * jnp.arange(1, HY_POS_BANDS + 1, dtype=F32)
    z = jnp.concatenate([t[:, None], jnp.cos(ang), jnp.sin(ang)], -1)
    z = jnp.pad(z, ((0, 0), (0, LANES - HY_EMB)))
    zg = jnp.concatenate([z[:1], jnp.flip(z[1:], 0)], 0)
    w1p = jnp.pad(w1, ((0, LANES - HY_EMB), (0, 0)))
    deltas = jnp.abs(jnp.linspace(math.log(HY_DECAY_TARGET) / HY_FAST_DECAY,
                                  math.log(HY_DECAY_TARGET) / HY_SLOW_DECAY, D_BRANCH, dtype=F32))
    tl = min(length, 512)
    full = lambda a: pl.BlockSpec(a.shape, lambda i: (0,) * a.ndim)
    args = [w1p, b1.reshape(1, -1), freq1.reshape(1, -1), w2, b2.reshape(1, -1), freq2.reshape(1, -1),
            w3, deltas.reshape(1, -1)]
    zspec = pl.BlockSpec((tl, LANES), lambda i: (i, 0))
    if grouped:
        nct, ng = D_BRANCH // LANES, FFT_N2 // FFT_G
        out_specs = [pl.BlockSpec((2, nct, ng, tl // ng, LANES), lambda i: (0, 0, 0, i, 0))]
        out_shape = [jax.ShapeDtypeStruct((2, nct, ng, length // ng, LANES), F32)]
    else:
        out_specs = [pl.BlockSpec((tl, D_BRANCH), lambda i: (i, 0))] * 2
        out_shape = [jax.ShapeDtypeStruct((length, D_BRANCH), F32)] * 2
    return pl.pallas_call(
        functools.partial(_filter_kernel, grouped=grouped),
        grid=(length // tl,),
        in_specs=[zspec, zspec] + [full(a) for a in args],
        out_specs=out_specs,
        out_shape=out_shape,
        compiler_params=_params("parallel"),
        name="hyena_filters",
    )(z, zg, *args)


def _fft_tables():
    n = FFT_N1 * FFT_N2
    half = FFT_N1 // 2
    k1 = np.arange(FFT_N1, dtype=np.float64)[:, None]
    n1 = np.arange(half, dtype=np.float64)[None, :]
    t1 = np.zeros((FFT_N2, 2 * FFT_N1, 2 * half))
    for n2 in range(FFT_N2):
        th = 2.0 * np.pi * k1 * (FFT_N2 * n1 + n2) / n
        gc, gs = np.cos(th), np.sin(th)
        t1[n2] = np.block([[gc, gs], [-gs, gc]])
    t1_inv = np.transpose(t1, (0, 2, 1)) / n
    sign = np.where(np.arange(2 * FFT_N1) % 2 == 0, 1.0, -1.0)[None, :, None]
    t1_filt = np.concatenate([t1[:, :, :half], sign * t1[:, :, :half]], axis=2)
    k2 = np.arange(FFT_N2, dtype=np.float64)
    th2 = 2.0 * np.pi * np.outer(k2, k2) / FFT_N2
    fc, fs = np.cos(th2), np.sin(th2)
    t2 = np.block([[fc, fs], [-fs, fc]])
    t2_inv = np.block([[fc, -fs], [fs, fc]])

    def hl(a):
        a32 = jnp.asarray(a, F32)
        hi = a32.astype(BF16)
        return hi, (a32 - hi.astype(F32)).astype(BF16)

    return hl(t1), hl(t1_inv), hl(t2), hl(t2_inv), hl(t1_filt)


TILES_PER_DOT = FFT_CT // LANES


def _lane_cat(parts):
    return parts[0] if len(parts) == 1 else jnp.concatenate(parts, axis=1)


def _fft1_kernel(z_ref, th_ref, tl_ref, a_ref):
    half = FFT_N1 // 2
    ntile = z_ref.shape[1]
    for j in range(FFT_G):
        rows = pl.ds(j, half, stride=FFT_G)
        for c in range(ntile // TILES_PER_DOT):
            tiles = range(c * TILES_PER_DOT, (c + 1) * TILES_PER_DOT)
            d = jnp.concatenate([_lane_cat([z_ref[s, t, 0, rows, :] for t in tiles]) for s in range(2)], axis=0)
            dh, dl = _split(d)
            a = _dot3_t(th_ref[j], tl_ref[j], dh, dl)
            for ri in range(2):
                for i, t in enumerate(tiles):
                    a_ref[0, 0, t, ri, pl.ds(j, FFT_N1, stride=FFT_G), :] = (
                        a[ri * FFT_N1:(ri + 1) * FFT_N1, i * LANES:(i + 1) * LANES])


def _fft_stage1(zg, t1):
    nseq, ntile, ng, rows, _ = zg.shape
    half = FFT_N1 // 2
    assert rows == half * FFT_G and ng * FFT_G == FFT_N2 and nseq % 2 == 0
    tspec = pl.BlockSpec((FFT_G, 2 * FFT_N1, 2 * half), lambda p, g: (g, 0, 0))
    return pl.pallas_call(
        _fft1_kernel,
        grid=(nseq // 2, ng),
        in_specs=[pl.BlockSpec((2, ntile, 1, rows, LANES), lambda p, g: (p, 0, g, 0, 0)), tspec, tspec],
        out_specs=pl.BlockSpec((1, 1, ntile, 2, FFT_G * FFT_N1, LANES), lambda p, g: (p, g, 0, 0, 0, 0)),
        out_shape=jax.ShapeDtypeStruct((nseq // 2, ng, ntile, 2, FFT_G * FFT_N1, LANES), F32),
        compiler_params=_params("parallel", "parallel"),
        name="fft_stage1",
    )(zg, *t1)


def _load_a(a_ref, k, tiles):
    ng = a_ref.shape[1]
    return jnp.concatenate(
        [_lane_cat([jnp.concatenate([a_ref[0, g, t, ri, k * FFT_G:(k + 1) * FFT_G, :] for g in range(ng)], axis=0)
                    for t in tiles]) for ri in range(2)], axis=0)


def _fft2_spec_kernel(a_ref, th_ref, tl_ref, k_ref):
    ntile = a_ref.shape[2]
    for k in range(FFT_KB):
        for c in range(ntile // TILES_PER_DOT):
            tiles = range(c * TILES_PER_DOT, (c + 1) * TILES_PER_DOT)
            dh, dl = _split(_load_a(a_ref, k, tiles))
            x = _dot3_t(th_ref[...], tl_ref[...], dh, dl)
            sl = slice(c * FFT_CT, (c + 1) * FFT_CT)
            k_ref[k, 0, :, sl] = x[:FFT_N2]
            k_ref[k, 1, :, sl] = x[FFT_N2:]


def _fft_spectrum(a_filt, t2):
    _, ng, ntile, _, _, _ = a_filt.shape
    ch = ntile * LANES
    tspec = pl.BlockSpec((2 * FFT_N2, 2 * FFT_N2), lambda i: (0, 0))
    return pl.pallas_call(
        _fft2_spec_kernel,
        grid=(FFT_N1 // FFT_KB,),
        in_specs=[pl.BlockSpec((1, ng, ntile, 2, FFT_KB * FFT_G, LANES), lambda i: (0, 0, 0, 0, i, 0)),
                  tspec, tspec],
        out_specs=pl.BlockSpec((FFT_KB, 2, FFT_N2, ch), lambda i: (i, 0, 0, 0)),
        out_shape=jax.ShapeDtypeStruct((FFT_N1, 2, FFT_N2, ch), F32),
        compiler_params=_params("parallel"),
        name="fft_spectrum",
    )(a_filt, *t2)


def _fft2_kernel(a_ref, k_ref, th_ref, tl_ref, ih_ref, il_ref, b_ref):
    ntile = a_ref.shape[2]
    for k in range(FFT_KB):
        for c in range(ntile // TILES_PER_DOT):
            tiles = range(c * TILES_PER_DOT, (c + 1) * TILES_PER_DOT)
            dh, dl = _split(_load_a(a_ref, k, tiles))
            x = _dot3_t(th_ref[...], tl_ref[...], dh, dl)
            xr, xi = x[:FFT_N2], x[FFT_N2:]
            sl = slice(c * FFT_CT, (c + 1) * FFT_CT)
            kr, ki = k_ref[k, 0, :, sl], k_ref[k, 1, :, sl]
            y = jnp.concatenate([xr * kr - xi * ki, xr * ki + xi * kr], axis=0)
            yh, yl = _split(y)
            out = _dot3_t(ih_ref[...], il_ref[...], yh, yl)
            for ri in range(2):
                for i, t in enumerate(tiles):
                    b_ref[0, 0, t, ri, pl.ds(k, FFT_N2, stride=FFT_KB), :] = (
                        out[ri * FFT_N2:(ri + 1) * FFT_N2, i * LANES:(i + 1) * LANES])


def _fft_stage2(a, kf, t2, t2_inv):
    npair, ng, ntile, _, _, _ = a.shape
    ch = ntile * LANES
    nkb = FFT_N1 // FFT_KB
    tspec = pl.BlockSpec((2 * FFT_N2, 2 * FFT_N2), lambda i, p: (0, 0))
    return pl.pallas_call(
        _fft2_kernel,
        grid=(nkb, npair),
        in_specs=[pl.BlockSpec((1, ng, ntile, 2, FFT_KB * FFT_G, LANES), lambda i, p: (p, 0, 0, 0, i, 0)),
                  pl.BlockSpec((FFT_KB, 2, FFT_N2, ch), lambda i, p: (i, 0, 0, 0)),
                  tspec, tspec, tspec, tspec],
        out_specs=pl.BlockSpec((1, 1, ntile, 2, FFT_KB * FFT_N2, LANES), lambda i, p: (p, i, 0, 0, 0, 0)),
        out_shape=jax.ShapeDtypeStruct((npair, nkb, ntile, 2, FFT_KB * FFT_N2, LANES), F32),
        compiler_params=_params("parallel", "parallel"),
        name="fft_stage2",
    )(a, kf, *t2, *t2_inv)


def _fft3_kernel(b_ref, th_ref, tl_ref, y_ref):
    half = FFT_N1 // 2
    nkb, ntile = b_ref.shape[1], b_ref.shape[2]
    for j in range(FFT_G):
        for c in range(ntile // TILES_PER_DOT):
            tiles = range(c * TILES_PER_DOT, (c + 1) * TILES_PER_DOT)
            d = jnp.concatenate(
                [_lane_cat([jnp.concatenate([b_ref[0, kb, t, ri, j * FFT_KB:(j + 1) * FFT_KB, :]
                                             for kb in range(nkb)], axis=0) for t in tiles])
                 for ri in range(2)], axis=0)
            dh, dl = _split(d)
            y = _dot3_t(th_ref[j], tl_ref[j], dh, dl)
            for s in range(2):
                for i, t in enumerate(tiles):
                    y_ref[s, t, 0, pl.ds(j, half, stride=FFT_G), :] = (
                        y[s * half:(s + 1) * half, i * LANES:(i + 1) * LANES])


def _fft_stage3(bm, t1_inv):
    npair, nkb, ntile, _, _, _ = bm.shape
    half = FFT_N1 // 2
    ng = FFT_N2 // FFT_G
    rows = half * FFT_G
    tspec = pl.BlockSpec((FFT_G, 2 * half, 2 * FFT_N1), lambda p, g: (g, 0, 0))
    return pl.pallas_call(
        _fft3_kernel,
        grid=(npair, ng),
        in_specs=[pl.BlockSpec((1, nkb, ntile, 2, FFT_KB * FFT_G, LANES), lambda p, g: (p, 0, 0, 0, g, 0)),
                  tspec, tspec],
        out_specs=pl.BlockSpec((2, ntile, 1, rows, LANES), lambda p, g: (p, 0, g, 0, 0)),
        out_shape=jax.ShapeDtypeStruct((2 * npair, ntile, ng, rows, LANES), F32),
        compiler_params=_params("parallel", "parallel"),
        name="fft_stage3",
    )(bm, *t1_inv)


def _hyena_long_conv(wg, filt_g, tables):
    t1, t1_inv, t2, t2_inv, t1_filt = tables
    kf = _fft_spectrum(_fft_stage1(filt_g, t1_filt), t2)
    a = _fft_stage1(wg, t1)
    bm = _fft_stage2(a, kf, t2, t2_inv)
    return _fft_stage3(bm, t1_inv)


def _dense_dft_tables(length):
    n = 2 * length
    k = np.arange(n, dtype=np.float64)
    th = 2.0 * np.pi * np.outer(k, k) / n
    c, s = np.cos(th), np.sin(th)
    t_full = np.concatenate([c, -s], 0)
    t_fwd = t_full[:, :length]
    t_inv = np.concatenate([c[:length], -s[:length]], 1) / n
    return tuple(jnp.asarray(a, F32) for a in (t_full, t_fwd, t_inv))


def _ctx_spec_kernel(t_ref, h_ref, o_ref):
    o_ref[...] = _dot3(t_ref[...], h_ref[...])


def _ctx_conv_kernel(w_ref, k_ref, tf_ref, ti_ref, o_ref):
    n = k_ref.shape[0] // 2
    x = _dot3(tf_ref[...], w_ref[0])
    xr, xi = x[:n], x[n:]
    kr, ki = k_ref[:n], k_ref[n:]
    y = jnp.concatenate([xr * kr - xi * ki, xr * ki + xi * kr], axis=0)
    o_ref[0] = _dot3(ti_ref[...], y)


def _ctx_long_conv(w, h_fwd, g):
    bsz, length, ch = w.shape
    n = 2 * length
    t_full, t_fwd, t_inv = _dense_dft_tables(length)
    kern = jnp.concatenate([h_fwd, g], 0)
    kf = pl.pallas_call(
        _ctx_spec_kernel,
        grid=(1,),
        in_specs=[pl.BlockSpec((2 * n, n), lambda i: (0, 0)), pl.BlockSpec((n, ch), lambda i: (0, 0))],
        out_specs=pl.BlockSpec((2 * n, ch), lambda i: (0, 0)),
        out_shape=jax.ShapeDtypeStruct((2 * n, ch), F32),
        compiler_params=_params("arbitrary"),
        name="ctx_spectrum",
    )(t_full, kern)
    return pl.pallas_call(
        _ctx_conv_kernel,
        grid=(bsz,),
        in_specs=[pl.BlockSpec((1, length, ch), lambda b: (b, 0, 0)),
                  pl.BlockSpec((2 * n, ch), lambda b: (0, 0)),
                  pl.BlockSpec((2 * n, length), lambda b: (0, 0)),
                  pl.BlockSpec((length, 2 * n), lambda b: (0, 0))],
        out_specs=pl.BlockSpec((1, length, ch), lambda b: (b, 0, 0)),
        out_shape=jax.ShapeDtypeStruct((bsz, length, ch), F32),
        compiler_params=_params("parallel"),
        name="ctx_long_conv",
    )(w, kf, t_fwd, t_inv)


def _merge_kernel(hf_ref, hb_ref, yb_ref, cv_ref, w_ref, x0_ref, ag_ref, bg_ref, cg_ref,
                  ga_ref, gb_ref, gc_ref, h_ref, gate_ref, wb_ref, wo_ref, ds_ref, lg_ref, lb_ref,
                  o_ref, *, alpha, grouped):
    ya = hf_ref[0].astype(F32) + hb_ref[0].astype(F32)
    if grouped:
        tm = o_ref.shape[1]
        w = _from_grouped(w_ref, (0,), tm)
        cv = _from_grouped(cv_ref, (0,), tm)
    else:
        w = w_ref[0]
        cv = cv_ref[0]
    yc = x0_ref[0] * (cv + ds_ref[...] * w)

    def gated(y, u_ref):
        u = u_ref[0].astype(F32)
        return ((y * u) * (1.0 + jnp.tanh(u))).astype(BF16)

    def sig2(u_ref):
        return 1.0 + jnp.tanh(u_ref[0].astype(F32))

    pa = _bdot(gated(ya, ag_ref), wb_ref[0])
    pb = _bdot(gated(yb_ref[0].astype(F32), bg_ref), wb_ref[1])
    pc = _bdot(gated(yc, cg_ref), wb_ref[2])
    m = sig2(ga_ref) * pa + sig2(gb_ref) * pb + sig2(gc_ref) * pc
    out = _bdot(m.astype(BF16), wo_ref[...])
    z = alpha * h_ref[0] + gate_ref[0] * out
    o_ref[0] = _layer_norm(z) * lg_ref[...] + lb_ref[...]


def _merge(proj, hf, hb, yb, conv, w, x0c, h, gate, wb_bf16, wo_bf16, d_skip, ln_g, ln_b, alpha, grouped):
    bsz, length, d = h.shape
    tm = min(length, 512)
    half = lambda: pl.BlockSpec((1, tm, D_BRANCH), lambda b, i: (b, i, 0))
    if grouped:
        ng = FFT_N2 // FFT_G
        hy = lambda: pl.BlockSpec((1, D_BRANCH // LANES, ng, tm // ng, LANES), lambda b, i: (b, 0, 0, i, 0))
    else:
        hy = half
    pcol = lambda col, width: pl.BlockSpec((1, tm, width), lambda b, i: (b, i, col // width))
    row = lambda n: pl.BlockSpec((1, n), lambda b, i: (0, 0))
    return pl.pallas_call(
        functools.partial(_merge_kernel, alpha=alpha, grouped=grouped),
        grid=(bsz, length // tm),
        in_specs=[half(), half(), half(), hy(), hy(), half(),
                  pcol(COL_LRU_G, D_BRANCH), pcol(COL_NA_G, D_BRANCH), pcol(COL_HY_G, D_BRANCH),
                  pcol(COL_GA, D_MODEL), pcol(COL_GB, D_MODEL), pcol(COL_GC, D_MODEL),
                  pl.BlockSpec((1, tm, d), lambda b, i: (b, i, 0)),
                  pl.BlockSpec((1, 1, d), lambda b, i: (b, 0, 0)),
                  pl.BlockSpec((3, D_BRANCH, d), lambda b, i: (0, 0, 0)),
                  pl.BlockSpec((d, d), lambda b, i: (0, 0)),
                  row(D_BRANCH), row(d), row(d)],
        out_specs=pl.BlockSpec((1, tm, d), lambda b, i: (b, i, 0)),
        out_shape=jax.ShapeDtypeStruct((bsz, length, d), F32),
        compiler_params=_params("parallel", "parallel"),
        name="merge",
    )(hf, hb, yb, conv, w, x0c, proj, proj, proj, proj, proj, proj, h, gate,
      wb_bf16, wo_bf16, d_skip.reshape(1, -1), ln_g.reshape(1, -1), ln_b.reshape(1, -1))


def _block_diag(w):
    g, n, _ = w.shape
    eye = jnp.eye(g, dtype=w.dtype)
    return (eye[:, None, :, None] * w[:, :, None, :]).reshape(g * n, g * n)


def kernel(x, c, ctx, c_ctx, w_mod, b_mod, w_in, lru_conv_w, lru_conv_b, lru_w_a, lru_b_a, lru_w_x, lru_b_x,
           lru_lambda, na_rpb, hy_conv_w, hy_conv_b, hy_w1, hy_b1, hy_freq1, hy_w2, hy_b2, hy_freq2, hy_w3,
           hy_d_skip, w_branch_out, w_out, ln_g, ln_b):
    depth = w_in.shape[0]
    bsz, seq_len, d = x.shape
    ctx_len = ctx.shape[1]
    alpha = (2 * depth) ** 0.25
    tables = _fft_tables()
    pad_rows = (-(bsz + 1)) % 8
    c_rows = jnp.concatenate([c, c_ctx[None, :], jnp.zeros((pad_rows, d), F32)], 0)

    col = np.arange(D_IN)
    is_gate = ((col >= COL_LRU_G) & (col < COL_Q)) | ((col >= COL_NA_G) & (col < COL_HY)) | (col >= COL_HY_G)
    gate_col_scale = jnp.asarray(np.where(is_gate, 0.5, 1.0)[None, :], F32)

    h, hc = x, ctx
    for l in range(depth):
        last = l == depth - 1
        mod = _modulation(c_rows, w_mod[l], b_mod[l])
        shift, scale, gate = (mod[:bsz, None, i * d:(i + 1) * d] for i in range(3))
        shift_x, scale_x, gate_x = (jnp.broadcast_to(mod[bsz, i * d:(i + 1) * d], (bsz, 1, d)) for i in range(3))
        gate, gate_x = 0.5 * gate, 0.5 * gate_x
        w_in_l = (w_in[l] * gate_col_scale).astype(BF16)
        proj = _in_projection(h, scale, shift, w_in_l)
        proj_x = _in_projection(hc.reshape(1, bsz * ctx_len, d), scale_x[:1], shift_x[:1], w_in_l)
        proj_x = proj_x.reshape(bsz, ctx_len, D_IN)

        w_cat = jnp.stack([jnp.concatenate([_block_diag(lru_w_a[l, dd]), _block_diag(lru_w_x[l, dd])], 1)
                           for dd in range(2)], 0)
        w_cat = w_cat.astype(BF16)
        b_a = 0.5 * lru_b_a[l][:, None, :]
        b_x = 0.5 * lru_b_x[l][:, None, :]
        sp = (-0.5 * LRU_C) * jax.nn.softplus(-lru_lambda[l])[:, None, :]
        xa = _lru_conv(proj, 0.5 * lru_conv_w[l], 0.5 * lru_conv_b[l])
        xa_x = _lru_conv(proj_x, 0.5 * lru_conv_w[l], 0.5 * lru_conv_b[l])
        hf_x, hb_x, h0 = _rglru(xa_x, w_cat, b_a, b_x, sp, jnp.zeros((2, bsz, D_BRANCH), F32))
        hf, hb, _ = _rglru(xa, w_cat, b_a, b_x, sp, h0)

        yb = _neighbourhood_attention(proj, proj_x, _na_bias(na_rpb[l], seq_len // GRID_W))

        filt_args = (hy_w1[l], hy_b1[l], hy_freq1[l], hy_w2[l], hy_b2[l], hy_freq2[l], hy_w3[l])
        x0c, w = _hyena_pre(proj, hy_conv_w[l], hy_conv_b[l], grouped=True)
        (filt_g,) = _hyena_filters(seq_len, *filt_args, grouped=True)
        conv = _hyena_long_conv(w, filt_g, tables)

        wb = w_branch_out[l].astype(BF16)
        wo = w_out[l].astype(BF16)
        h_new = _merge(proj, hf, hb, yb, conv, w, x0c, h, gate, wb, wo, hy_d_skip[l], ln_g[l], ln_b[l], alpha,
                       grouped=True)

        if not last:
            yb_x = _context_attention(proj_x)
            x0c_x, w_x = _hyena_pre(proj_x, hy_conv_w[l], hy_conv_b[l], grouped=False)
            hx_fwd, hx_g = _hyena_filters(ctx_len, *filt_args, grouped=False)
            conv_x = _ctx_long_conv(w_x, hx_fwd, hx_g)
            hc = _merge(proj_x, hf_x, hb_x, yb_x, conv_x, w_x, x0c_x, hc, gate_x, wb, wo, hy_d_skip[l],
                        ln_g[l], ln_b[l], alpha, grouped=False)
        h = h_new
    return h
```
